```python
import math
import jax, jax.numpy as jnp
from jax import lax
import numpy as np

D_MODEL = 2048
BATCH = 1
SEQ = 16384
DEPTH = 2

Q_BLOCK = 128
ROPE_THETA = 500000.0
NORM_EPS = 1e-5

FOX_HEADS = 4
FOX_HEAD_DIM = 128
DIFF_HEADS = 4
DIFF_HEAD_DIM = 64
DIFF_V_DIM = 2 * DIFF_HEAD_DIM
DIFF_ROT_DIM = DIFF_HEAD_DIM // 4
MLA_HEADS = 4
MLA_Q_RANK = 512
MLA_KV_RANK = 512
MLA_NOPE_DIM = 128
MLA_ROPE_DIM = 64
MLA_V_DIM = 128
MLA_QK_DIM = MLA_NOPE_DIM + MLA_ROPE_DIM

FOX_WIDTH = FOX_HEADS * FOX_HEAD_DIM
DIFF_QK_WIDTH = DIFF_HEADS * 2 * DIFF_HEAD_DIM
DIFF_WIDTH = DIFF_HEADS * DIFF_V_DIM
MLA_WIDTH = MLA_HEADS * MLA_V_DIM
N_BRANCHES = 3

IN_SPLIT_SIZES = (FOX_WIDTH, FOX_WIDTH, FOX_WIDTH, FOX_HEADS,
                  DIFF_QK_WIDTH, DIFF_QK_WIDTH, DIFF_WIDTH,
                  MLA_Q_RANK, MLA_KV_RANK, MLA_ROPE_DIM,
                  N_BRANCHES * D_MODEL)
IN_WIDTH = sum(IN_SPLIT_SIZES)

N_EXPERTS = 32
N_GROUPS = 8
EXPERTS_PER_GROUP = N_EXPERTS // N_GROUPS
TOP_K = 2
D_FF_EXPERT = 1408
MOE_ROW_BLOCK = 256

DEEPNORM_ALPHA = (2 * DEPTH) ** 0.25
DEEPNORM_BETA = (8 * DEPTH) ** -0.25

kernel_name = "hybrid_fox_diff_mla_gated_deepnorm_group_moe"


def _split(t, sizes):
    offs = np.cumsum(sizes)[:-1].tolist()
    return jnp.split(t, offs, axis=-1)


def _heads(t, n):
    b, s, _ = t.shape
    return t.reshape(b, s, n, -1).transpose(0, 2, 1, 3)


def _merge_heads(t):
    b, n, s, d = t.shape
    return t.transpose(0, 2, 1, 3).reshape(b, s, n * d)


def _layer_norm(x, g, b):
    xf = x.astype(jnp.float32)
    mu = jnp.mean(xf, axis=-1, keepdims=True)
    var = jnp.mean(jnp.square(xf - mu), axis=-1, keepdims=True)
    return ((xf - mu) * lax.rsqrt(var + NORM_EPS) * g + b).astype(x.dtype)


def _rms_norm(x, g):
    xf = x.astype(jnp.float32)
    ms = jnp.mean(jnp.square(xf), axis=-1, keepdims=True)
    return (xf * lax.rsqrt(ms + NORM_EPS) * g).astype(x.dtype)


def _rope_tables(seq, dim):
    pos = jnp.arange(seq, dtype=jnp.float32)
    inv_freq = ROPE_THETA ** (-jnp.arange(0, dim, 2, dtype=jnp.float32) / dim)
    ang = pos[:, None] * inv_freq[None, :]
    return jnp.cos(ang), jnp.sin(ang)


def _rope(x, cos, sin):
    half = x.shape[-1] // 2
    x1 = x[..., :half].astype(jnp.float32)
    x2 = x[..., half:].astype(jnp.float32)
    return jnp.concatenate([x1 * cos - x2 * sin, x2 * cos + x1 * sin], axis=-1).astype(x.dtype)


def _partial_rope(x, cos, sin, rot_dim):
    return jnp.concatenate([_rope(x[..., :rot_dim], cos, sin), x[..., rot_dim:]], axis=-1)


def _causal_block_attention(q, k, v, scale, log_forget=None):
    b, h, s, dk = q.shape
    dv = v.shape[-1]
    nb = s // Q_BLOCK
    kf = k.astype(jnp.float32)
    vf = v.astype(jnp.float32)
    key_pos = jnp.arange(s)
    q_blocks = q.reshape(b, h, nb, Q_BLOCK, dk).transpose(2, 0, 1, 3, 4)
    blk_ids = jnp.arange(nb)
    if log_forget is None:
        xs = (q_blocks, blk_ids)
    else:
        cum = jnp.cumsum(log_forget.astype(jnp.float32), axis=-1)
        cum_blocks = cum.reshape(b, h, nb, Q_BLOCK).transpose(2, 0, 1, 3)
        xs = (q_blocks, blk_ids, cum_blocks)

    def one_block(args):
        qi, i = args[0], args[1]
        logits = jnp.einsum("bhqd,bhkd->bhqk", qi.astype(jnp.float32), kf) * scale
        if log_forget is not None:
            logits = logits + (args[2][..., :, None] - cum[:, :, None, :])
        q_pos = i * Q_BLOCK + jnp.arange(Q_BLOCK)
        causal = key_pos[None, :] <= q_pos[:, None]
        logits = jnp.where(causal, logits, -jnp.inf)
        probs = jax.nn.softmax(logits, axis=-1)
        return jnp.einsum("bhqk,bhkd->bhqd", probs, vf)

    out = lax.map(one_block, xs)
    return out.transpose(1, 2, 0, 3, 4).reshape(b, h, s, dv).astype(v.dtype)


def _token_mixers(x, w_in, b_fox_f, b_gate, diff_lambda, g_diff, g_mla_q, g_mla_kv,
                  w_mla_uq, w_mla_ukv, w_fox_up, w_diff_up, w_mla_up, w_o,
                  layer_idx, rope_part, rope_mla):
    b, s, _ = x.shape
    (fox_q, fox_k, fox_v, fox_f, diff_q, diff_k, diff_v,
     mla_cq, mla_ckv, mla_kpe, gate_logits) = _split(x @ w_in, IN_SPLIT_SIZES)

    log_f = jax.nn.log_sigmoid((fox_f + b_fox_f).astype(jnp.float32)).transpose(0, 2, 1)
    o_fox = _causal_block_attention(_heads(fox_q, FOX_HEADS), _heads(fox_k, FOX_HEADS),
                                    _heads(fox_v, FOX_HEADS), FOX_HEAD_DIM ** -0.5, log_f)
    y_fox = _merge_heads(o_fox) @ w_fox_up

    cos_p, sin_p = rope_part
    dq = _partial_rope(_heads(diff_q, 2 * DIFF_HEADS), cos_p, sin_p, DIFF_ROT_DIM)
    dk = _partial_rope(_heads(diff_k, 2 * DIFF_HEADS), cos_p, sin_p, DIFF_ROT_DIM)
    dv = jnp.repeat(_heads(diff_v, DIFF_HEADS), 2, axis=1)
    o_pair = _causal_block_attention(dq, dk, dv, DIFF_HEAD_DIM ** -0.5)
    o_pair = o_pair.reshape(b, DIFF_HEADS, 2, s, DIFF_V_DIM)
    lam_init = 0.8 - 0.6 * math.exp(-0.3 * layer_idx)
    lam_p = diff_lambda.astype(jnp.float32)
    lam = jnp.exp(jnp.sum(lam_p[0] * lam_p[1])) - jnp.exp(jnp.sum(lam_p[2] * lam_p[3])) + lam_init
    o_diff = o_pair[:, :, 0].astype(jnp.float32) - lam * o_pair[:, :, 1].astype(jnp.float32)
    o_diff = (_rms_norm(o_diff, g_diff) * (1.0 - lam_init)).astype(x.dtype)
    y_diff = _merge_heads(o_diff) @ w_diff_up

    cos_m, sin_m = rope_mla
    q_all = _heads(_rms_norm(mla_cq, g_mla_q) @ w_mla_uq, MLA_HEADS)
    q_nope, q_pe = q_all[..., :MLA_NOPE_DIM], _rope(q_all[..., MLA_NOPE_DIM:], cos_m, sin_m)
    kv = _heads(_rms_norm(mla_ckv, g_mla_kv) @ w_mla_ukv, MLA_HEADS)
    k_nope, v_mla = kv[..., :MLA_NOPE_DIM], kv[..., MLA_NOPE_DIM:]
    k_pe = jnp.broadcast_to(_rope(mla_kpe[:, None], cos_m, sin_m), (b, MLA_HEADS, s, MLA_ROPE_DIM))
    o_mla = _causal_block_attention(jnp.concatenate([q_nope, q_pe], axis=-1),
                                    jnp.concatenate([k_nope, k_pe], axis=-1),
                                    v_mla, MLA_QK_DIM ** -0.5)
    y_mla = _merge_heads(o_mla) @ w_mla_up

    g_fox, g_diff_b, g_mla = _split(jax.nn.sigmoid(gate_logits + b_gate), (D_MODEL,) * N_BRANCHES)
    merged = g_fox * y_fox + g_diff_b * y_diff + g_mla * y_mla
    return (merged @ w_o).astype(x.dtype)


def _route(h_flat, w_router, router_bias):
    t = h_flat.shape[0]
    scores = jax.nn.sigmoid((h_flat @ w_router).astype(jnp.float32))
    biased = scores + router_bias.astype(jnp.float32)
    grouped = biased.reshape(t, N_GROUPS, EXPERTS_PER_GROUP)
    group_score = lax.top_k(grouped, TOP_K)[0].sum(-1)
    best_group = jnp.argmax(group_score, axis=-1)
    in_group = grouped[jnp.arange(t), best_group]
    _, local = lax.top_k(in_group, TOP_K)
    expert = best_group[:, None] * EXPERTS_PER_GROUP + local
    w = jnp.take_along_axis(scores, expert, axis=1)
    return expert, w / jnp.sum(w, axis=-1, keepdims=True)


def _moe(h, w_router, router_bias, w_exp_in, w_exp_out):
    b, s, d = h.shape
    t = b * s
    h_flat = h.reshape(t, d)
    expert, weight = _route(h_flat, w_router, router_bias)
    flat_e = expert.reshape(-1)
    flat_w = weight.reshape(-1)
    n = t * TOP_K
    order = jnp.argsort(flat_e)
    sorted_e = flat_e[order]
    counts = jnp.bincount(flat_e, length=N_EXPERTS)
    starts = jnp.cumsum(counts) - counts
    padded = ((counts + MOE_ROW_BLOCK - 1) // MOE_ROW_BLOCK) * MOE_ROW_BLOCK
    padded_ends = jnp.cumsum(padded)
    padded_starts = padded_ends - padded
    dest = padded_starts[sorted_e] + jnp.arange(n) - starts[sorted_e]
    n_blocks = (n + N_EXPERTS * (MOE_ROW_BLOCK - 1) + MOE_ROW_BLOCK - 1) // MOE_ROW_BLOCK
    n_rows = n_blocks * MOE_ROW_BLOCK
    row_tok = jnp.full((n_rows,), t, jnp.int32).at[dest].set((order // TOP_K).astype(jnp.int32))
    row_w = jnp.zeros((n_rows,), jnp.float32).at[dest].set(flat_w[order])
    block_e = jnp.minimum(jnp.searchsorted(padded_ends, jnp.arange(n_blocks) * MOE_ROW_BLOCK, side="right"),
                          N_EXPERTS - 1)
    h_pad = jnp.concatenate([h_flat, jnp.zeros((1, d), h.dtype)], axis=0)

    def one_block(args):
        tok, wt, e = args
        xb = h_pad[tok]
        gate, up = jnp.split(xb @ w_exp_in[e], 2, axis=-1)
        y = (jax.nn.silu(gate) * up) @ w_exp_out[e]
        return (y * wt[:, None]).astype(h.dtype)

    y = lax.map(one_block, (row_tok.reshape(n_blocks, MOE_ROW_BLOCK),
                            row_w.reshape(n_blocks, MOE_ROW_BLOCK), block_e))
    out = jnp.zeros((t + 1, d), h.dtype).at[row_tok].add(y.reshape(n_rows, d))[:t]
    return out.reshape(b, s, d)


def setup_inputs(seed: int = 0) -> dict:
    key = jax.random.key(seed)
    ks = jax.random.split(key, 24)
    f32 = jnp.float32
    nrm = lambda k, shape, scale: jax.random.normal(k, shape, f32) * scale
    return {
        "x": nrm(ks[0], (BATCH, SEQ, D_MODEL), 1.0),
        "w_in": nrm(ks[1], (DEPTH, D_MODEL, IN_WIDTH), D_MODEL ** -0.5),
        "b_fox_f": nrm(ks[2], (DEPTH, FOX_HEADS), 0.1),
        "b_gate": nrm(ks[3], (DEPTH, N_BRANCHES * D_MODEL), 0.02),
        "diff_lambda": nrm(ks[4], (DEPTH, 4, DIFF_HEAD_DIM), 0.1),
        "g_diff": 1.0 + nrm(ks[5], (DEPTH, DIFF_V_DIM), 0.02),
        "g_mla_q": 1.0 + nrm(ks[6], (DEPTH, MLA_Q_RANK), 0.02),
        "g_mla_kv": 1.0 + nrm(ks[7], (DEPTH, MLA_KV_RANK), 0.02),
        "w_mla_uq": nrm(ks[8], (DEPTH, MLA_Q_RANK, MLA_HEADS * MLA_QK_DIM), MLA_Q_RANK ** -0.5),
        "w_mla_ukv": nrm(ks[9], (DEPTH, MLA_KV_RANK, MLA_HEADS * (MLA_NOPE_DIM + MLA_V_DIM)), MLA_KV_RANK ** -0.5),
        "w_fox_up": nrm(ks[10], (DEPTH, FOX_WIDTH, D_MODEL), DEEPNORM_BETA * FOX_WIDTH ** -0.5),
        "w_diff_up": nrm(ks[11], (DEPTH, DIFF_WIDTH, D_MODEL), DEEPNORM_BETA * DIFF_WIDTH ** -0.5),
        "w_mla_up": nrm(ks[12], (DEPTH, MLA_WIDTH, D_MODEL), DEEPNORM_BETA * MLA_WIDTH ** -0.5),
        "w_o": nrm(ks[13], (DEPTH, D_MODEL, D_MODEL), DEEPNORM_BETA * D_MODEL ** -0.5),
        "ln1_g": 1.0 + nrm(ks[14], (DEPTH, D_MODEL), 0.02),
        "ln1_b": nrm(ks[15], (DEPTH, D_MODEL), 0.02),
        "ln2_g": 1.0 + nrm(ks[16], (DEPTH, D_MODEL), 0.02),
        "ln2_b": nrm(ks[17], (DEPTH, D_MODEL), 0.02),
        "w_router": nrm(ks[18], (D_MODEL, N_EXPERTS), D_MODEL ** -0.5),
        "router_bias": nrm(ks[19], (N_EXPERTS,), 0.01),
        "w_exp_in": nrm(ks[20], (DEPTH, N_EXPERTS, D_MODEL, 2 * D_FF_EXPERT), D_MODEL ** -0.5),
        "w_exp_out": nrm(ks[21], (DEPTH, N_EXPERTS, D_FF_EXPERT, D_MODEL), DEEPNORM_BETA * D_FF_EXPERT ** -0.5),
    }


def reference(x, w_in, b_fox_f, b_gate, diff_lambda, g_diff, g_mla_q, g_mla_kv, w_mla_uq, w_mla_ukv,
              w_fox_up, w_diff_up, w_mla_up, w_o, ln1_g, ln1_b, ln2_g, ln2_b,
              w_router, router_bias, w_exp_in, w_exp_out):
    s = x.shape[1]
    rope_part = _rope_tables(s, DIFF_ROT_DIM)
    rope_mla = _rope_tables(s, MLA_ROPE_DIM)
    for i in range(DEPTH):
        mix = _token_mixers(x, w_in[i], b_fox_f[i], b_gate[i], diff_lambda[i], g_diff[i],
                            g_mla_q[i], g_mla_kv[i], w_mla_uq[i], w_mla_ukv[i],
                            w_fox_up[i], w_diff_up[i], w_mla_up[i], w_o[i],
                            i, rope_part, rope_mla)
        x = _layer_norm(DEEPNORM_ALPHA * x + mix, ln1_g[i], ln1_b[i])
        ffn = _moe(x, w_router, router_bias, w_exp_in[i], w_exp_out[i])
        x = _layer_norm(DEEPNORM_ALPHA * x + ffn, ln2_g[i], ln2_b[i])
    return x
```

```python
import functools
import math

import numpy as np
import jax
import jax.numpy as jnp
from jax import lax
from jax.experimental import pallas as pl
from jax.experimental.pallas import tpu as pltpu

F32 = jnp.float32
BF16 = jnp.bfloat16

D_MODEL = 2048
SEQ = 16384
DEPTH = 2
ROPE_THETA = 500000.0
NORM_EPS = 1e-5
FOX_HEADS = 4
FOX_HEAD_DIM = 128
DIFF_HEADS = 4
DIFF_HEAD_DIM = 64
DIFF_V_DIM = 128
DIFF_ROT_DIM = 16
MLA_HEADS = 4
MLA_Q_RANK = 512
MLA_KV_RANK = 512
MLA_NOPE_DIM = 128
MLA_ROPE_DIM = 64
MLA_V_DIM = 128
MLA_QK_DIM = MLA_NOPE_DIM + MLA_ROPE_DIM
N_EXPERTS = 32
N_GROUPS = 8
EXPERTS_PER_GROUP = 4
TOP_K = 2
D_FF_EXPERT = 1408
DEEPNORM_ALPHA = (2 * DEPTH) ** 0.25

LANES = 128
ROW_CHUNKS = D_MODEL // LANES
MLA_QK_PAD = 256
VMEM_LIMIT = 56 * 1024 * 1024

LIN_TM = 1024
LIN_TN = 1024
PREP_TM = 512
CUM_T = 512
ATT_T = 512
POST_TM = 256
RANK_T = 512
DEST_T = 2048
DISP_T = 512
MOE_R = 256
MOE_NB = (SEQ * TOP_K + N_EXPERTS * (MOE_R - 1) + MOE_R - 1) // MOE_R
MOE_ROWS = MOE_NB * MOE_R
BLK_LANES = 256
COMB_T = 256
NEG = -1e30


def _cparams(n_axes, **kw):
    return pltpu.CompilerParams(dimension_semantics=("arbitrary",) * n_axes,
                                vmem_limit_bytes=VMEM_LIMIT, **kw)


def _dot(a, b):
    return jnp.dot(a, b, preferred_element_type=F32)


def _dot_nt(a, b):
    return lax.dot_general(a, b, (((1,), (1,)), ((), ())), preferred_element_type=F32)


def _split3(x):
    a = x.astype(BF16)
    r = x - a.astype(F32)
    b = r.astype(BF16)
    c = (r - b.astype(F32)).astype(BF16)
    return a, b, c


def _linear_kernel(x_ref, w_ref, b_ref, o_ref, xb_ref, *, act):
    @pl.when(pl.program_id(1) == 0)
    def _():
        xb_ref[...] = x_ref[...].astype(BF16)

    acc = _dot(xb_ref[...], w_ref[...]) + b_ref[...]
    if act == "sigmoid":
        acc = jax.nn.sigmoid(acc)
    o_ref[...] = acc.astype(o_ref.dtype)


def _linear(x, w, b, act, out_dtype, tn, name):
    m, k = x.shape
    n = w.shape[1]
    return pl.pallas_call(
        functools.partial(_linear_kernel, act=act),
        grid=(m // LIN_TM, n // tn),
        in_specs=[pl.BlockSpec((LIN_TM, k), lambda i, j: (i, 0)),
                  pl.BlockSpec((k, tn), lambda i, j: (0, j)),
                  pl.BlockSpec((1, tn), lambda i, j: (0, j))],
        out_specs=pl.BlockSpec((LIN_TM, tn), lambda i, j: (i, j)),
        out_shape=jax.ShapeDtypeStruct((m, n), out_dtype),
        scratch_shapes=[pltpu.VMEM((LIN_TM, k), BF16)],
        compiler_params=_cparams(2),
        name=name,
    )(x, w, b)


def _rms(xf, g):
    ms = jnp.mean(jnp.square(xf), axis=-1, keepdims=True)
    return xf * lax.rsqrt(ms + NORM_EPS) * g


def _rope(x_f32, x_b16, r_ref, c, s):
    return x_f32 * c + _dot(x_b16, r_ref[...]) * s


def _prep_kernel(dq_ref, dk_ref, cq_ref, ckv_ref, ps_ref, cd_ref, sd_ref, cmq_ref, smq_ref, ckp_ref, skp_ref,
                 gq_ref, gkv_ref, wuq_ref, wukv_ref, rd_ref, rmq_ref, rkp_ref, bf_ref,
                 dqo_ref, dko_ref, mq_ref, mk_ref, mv_ref, lft_ref):
    cd = cd_ref[...]
    sd = sd_ref[...]
    dq = dq_ref[...]
    dk = dk_ref[...]
    dqo_ref[...] = (_rope(dq.astype(F32), dq, rd_ref, cd, sd) * (DIFF_HEAD_DIM ** -0.5)).astype(BF16)
    dko_ref[...] = _rope(dk.astype(F32), dk, rd_ref, cd, sd).astype(BF16)

    cqn = _rms(cq_ref[...].astype(F32), gq_ref[...]).astype(BF16)
    q = _dot(cqn, wuq_ref[...])
    mq_ref[...] = _rope(q, q.astype(BF16), rmq_ref, cmq_ref[...], smq_ref[...]).astype(BF16)

    ckvn = _rms(ckv_ref[...].astype(F32), gkv_ref[...]).astype(BF16)
    kv = _dot(ckvn, wukv_ref[...])
    ps = ps_ref[...]
    kpe = _rope(ps, ps.astype(BF16), rkp_ref, ckp_ref[...], skp_ref[...]).astype(BF16)
    for h in range(MLA_HEADS):
        mk_ref[:, h * MLA_QK_PAD:h * MLA_QK_PAD + LANES] = kv[:, h * LANES:(h + 1) * LANES].astype(BF16)
        mk_ref[:, h * MLA_QK_PAD + LANES:(h + 1) * MLA_QK_PAD] = kpe
    mv_ref[...] = kv[:, MLA_HEADS * LANES:].astype(BF16)

    z = ps + bf_ref[...]
    lf = jnp.minimum(z, 0.0) - jnp.log1p(jnp.exp(-jnp.abs(z)))
    lft_ref[...] = lf.T[MLA_ROPE_DIM:MLA_ROPE_DIM + 8, :]


def _prep(pa, ps, tabs, gq, gkv, wuq, wukv, rmats, bf):
    s = pa.shape[0]
    tm = PREP_TM
    row = lambda w: pl.BlockSpec((tm, w), lambda i: (i, 0))
    col = lambda w, c: pl.BlockSpec((tm, w), lambda i: (i, c))
    full = lambda a: pl.BlockSpec(a.shape, lambda i: (0,) * a.ndim)
    cd, sd, cmq, smq, ckp, skp = tabs
    rd, rmq, rkp = rmats
    return pl.pallas_call(
        _prep_kernel,
        grid=(s // tm,),
        in_specs=[col(512, 3), col(512, 4), col(512, 6), col(512, 7), row(LANES),
                  row(512), row(512), row(1024), row(1024), row(LANES), row(LANES),
                  full(gq), full(gkv), full(wuq), full(wukv), full(rd), full(rmq), full(rkp), full(bf)],
        out_specs=[row(512), row(512), row(1024), row(1024), row(512),
                   pl.BlockSpec((8, tm), lambda i: (0, i))],
        out_shape=[jax.ShapeDtypeStruct((s, 512), BF16), jax.ShapeDtypeStruct((s, 512), BF16),
                   jax.ShapeDtypeStruct((s, 1024), BF16), jax.ShapeDtypeStruct((s, 1024), BF16),
                   jax.ShapeDtypeStruct((s, 512), BF16), jax.ShapeDtypeStruct((8, s), F32)],
        compiler_params=_cparams(1),
        name="prep",
    )(pa, pa, pa, pa, ps, cd, sd, cmq, smq, ckp, skp, gq, gkv, wuq, wukv, rd, rmq, rkp, bf)


def _cumsum_kernel(x_ref, u_ref, o_ref, carry_ref):
    @pl.when(pl.program_id(0) == 0)
    def _():
        carry_ref[...] = jnp.zeros_like(carry_ref)

    a, b, c = _split3(x_ref[...])
    u = u_ref[...]
    cs = _dot(a, u) + _dot(b, u) + _dot(c, u) + carry_ref[:, :1]
    o_ref[...] = cs
    carry_ref[...] = jnp.broadcast_to(cs[:, CUM_T - 1:CUM_T], carry_ref.shape)


def _cumsum(x, u):
    s = x.shape[1]
    return pl.pallas_call(
        _cumsum_kernel,
        grid=(s // CUM_T,),
        in_specs=[pl.BlockSpec((8, CUM_T), lambda i: (0, i)), pl.BlockSpec(u.shape, lambda i: (0, 0))],
        out_specs=pl.BlockSpec((8, CUM_T), lambda i: (0, i)),
        out_shape=jax.ShapeDtypeStruct((8, s), F32),
        scratch_shapes=[pltpu.VMEM((8, LANES), F32)],
        compiler_params=_cparams(1),
        name="cumsum",
    )(x, u)


def _flash_kernel(*refs, mode, lam_init):
    t = ATT_T
    if mode == "fox":
        q_ref, k_ref, v_ref, ck_ref, o_ref, m_ref, l_ref, acc_ref = refs
    elif mode == "diff":
        q_ref, k_ref, v_ref, lam_ref, gd_ref, o_ref, m_ref, l_ref, acc_ref = refs
    else:
        q_ref, k_ref, v_ref, o_ref, m_ref, l_ref, acc_ref = refs
    i = pl.program_id(1)
    off_i = pl.multiple_of(i * t, t)
    m_ref[...] = jnp.full(m_ref.shape, NEG, F32)
    l_ref[...] = jnp.zeros_like(l_ref)
    acc_ref[...] = jnp.zeros_like(acc_ref)

    q = q_ref[...]
    if mode == "fox":
        qs = [(q.astype(F32) * (FOX_HEAD_DIM ** -0.5)).astype(BF16)]
        c_ref = ck_ref[:, pl.ds(off_i, LANES)][:, :1]
    elif mode == "diff":
        lane = lax.broadcasted_iota(jnp.int32, q.shape, 1)
        zero = jnp.zeros_like(q)
        qs = [jnp.where(lane < DIFF_HEAD_DIM, q, zero), jnp.where(lane >= DIFF_HEAD_DIM, q, zero)]
    else:
        qs = [q]

    def step(off, masked):
        kb = k_ref[pl.ds(off, t), :]
        vb = v_ref[pl.ds(off, t), :]
        for n, qn in enumerate(qs):
            s = _dot_nt(qn, kb)
            if mode == "fox":
                s = s + (c_ref - ck_ref[:, pl.ds(off, t)])
            if masked:
                r_id = lax.broadcasted_iota(jnp.int32, s.shape, 0)
                c_id = lax.broadcasted_iota(jnp.int32, s.shape, 1)
                s = jnp.where(r_id >= c_id, s, NEG)
            m_prev = m_ref[n]
            m_new = jnp.maximum(m_prev, jnp.max(s, axis=1, keepdims=True))
            alpha = jnp.exp(m_prev - m_new)
            p = jnp.exp(s - m_new)
            l_ref[n] = alpha * l_ref[n] + jnp.sum(p, axis=1, keepdims=True)
            acc_ref[n] = alpha * acc_ref[n] + _dot(p.astype(BF16), vb)
            m_ref[n] = m_new

    def body(j, carry):
        step(pl.multiple_of(j * t, t), False)
        return carry

    lax.fori_loop(0, i, body, 0)
    step(off_i, True)

    if mode == "diff":
        o0 = acc_ref[0] / l_ref[0]
        o1 = acc_ref[1] / l_ref[1]
        dl = lam_ref[...]
        lam = (jnp.exp(jnp.sum(dl[0:1] * dl[1:2], axis=1, keepdims=True))
               - jnp.exp(jnp.sum(dl[2:3] * dl[3:4], axis=1, keepdims=True)) + lam_init)
        o = o0 - lam * o1
        o = _rms(o, gd_ref[...]) * (1.0 - lam_init)
    else:
        o = acc_ref[0] / l_ref[0]
    o_ref[...] = o.astype(o_ref.dtype)


def _flash(mode, q_arr, q_blk, k_arr, k_blk, v_arr, v_blk, dk, heads, extra=(), lam_init=0.0):
    s = q_arr.shape[0]
    t = ATT_T
    dv = LANES
    n_stream = 2 if mode == "diff" else 1
    in_specs = [pl.BlockSpec((t, dk), lambda h, i: (i, q_blk + h)),
                pl.BlockSpec((s, dk), lambda h, i: (0, k_blk + h)),
                pl.BlockSpec((s, dv), lambda h, i: (0, v_blk + h))]
    if mode == "fox":
        in_specs.append(pl.BlockSpec((None, 1, s), lambda h, i: (h, 0, 0)))
    elif mode == "diff":
        in_specs += [pl.BlockSpec(extra[0].shape, lambda h, i: (0, 0)),
                     pl.BlockSpec(extra[1].shape, lambda h, i: (0, 0))]
    return pl.pallas_call(
        functools.partial(_flash_kernel, mode=mode, lam_init=lam_init),
        grid=(heads, s // t),
        in_specs=in_specs,
        out_specs=pl.BlockSpec((t, dv), lambda h, i: (i, h)),
        out_shape=jax.ShapeDtypeStruct((s, heads * dv), BF16),
        scratch_shapes=[pltpu.VMEM((n_stream, t, 1), F32), pltpu.VMEM((n_stream, t, 1), F32),
                        pltpu.VMEM((n_stream, t, dv), F32)],
        compiler_params=_cparams(2),
        name="flash_" + mode,
    )(q_arr, k_arr, v_arr, *extra)


def _layer_norm(z, g, b):
    mu = jnp.mean(z, axis=-1, keepdims=True)
    zc = z - mu
    var = jnp.mean(jnp.square(zc), axis=-1, keepdims=True)
    return zc * lax.rsqrt(var + NORM_EPS) * g + b


def _first_index_of_max(vals):
    best = vals[0]
    for v in vals[1:]:
        best = jnp.maximum(best, v)
    idx = jnp.full(best.shape, len(vals) - 1, jnp.int32)
    for j in range(len(vals) - 2, -1, -1):
        idx = jnp.where(vals[j] == best, j, idx)
    return best, idx


def _post_kernel(of_ref, od_ref, om_ref, g0_ref, g1_ref, g2_ref, x_ref, wf_ref, wd_ref, wm_ref, wo_ref,
                 lg_ref, lb_ref, wr_ref, rb_ref, x1_ref, h16_ref, eid_ref, wt_ref):
    tm = POST_TM
    merged = (g0_ref[...].astype(F32) * _dot(of_ref[...], wf_ref[...])
              + g1_ref[...].astype(F32) * _dot(od_ref[...], wd_ref[...])
              + g2_ref[...].astype(F32) * _dot(om_ref[...], wm_ref[...]))
    mix = _dot(merged.astype(BF16), wo_ref[...])
    x1 = _layer_norm(DEEPNORM_ALPHA * x_ref[...] + mix, lg_ref[...], lb_ref[...])
    x1_ref[...] = x1
    for c in range(ROW_CHUNKS):
        h16_ref[pl.ds(c, tm, stride=ROW_CHUNKS), :] = x1[:, c * LANES:(c + 1) * LANES]

    logits = lax.dot_general(wr_ref[...], x1, (((1,), (1,)), ((), ())),
                             precision=lax.Precision.HIGHEST, preferred_element_type=F32)
    scores = jax.nn.sigmoid(logits)
    biased = scores + rb_ref[...]
    a = [biased[j * 8:(j + 1) * 8] for j in range(EXPERTS_PER_GROUP)]
    u = [scores[j * 8:(j + 1) * 8] for j in range(EXPERTS_PER_GROUP)]
    gs = a[0] + a[1]
    for j0 in range(EXPERTS_PER_GROUP):
        for j1 in range(j0 + 1, EXPERTS_PER_GROUP):
            gs = jnp.maximum(gs, a[j0] + a[j1])
    gmax = jnp.max(gs, axis=0, keepdims=True)
    gid = lax.broadcasted_iota(jnp.int32, gs.shape, 0)
    best = jnp.min(jnp.where(gs == gmax, gid, N_GROUPS), axis=0, keepdims=True)
    sel = gid == best
    ing = [jnp.sum(jnp.where(sel, a[j], 0.0), axis=0, keepdims=True) for j in range(EXPERTS_PER_GROUP)]
    unb = [jnp.sum(jnp.where(sel, u[j], 0.0), axis=0, keepdims=True) for j in range(EXPERTS_PER_GROUP)]
    _, l1 = _first_index_of_max(ing)
    ing2 = [jnp.where(l1 == j, -jnp.inf, ing[j]) for j in range(EXPERTS_PER_GROUP)]
    _, l2 = _first_index_of_max(ing2)
    w1 = jnp.zeros_like(unb[0])
    w2 = jnp.zeros_like(unb[0])
    for j in range(EXPERTS_PER_GROUP):
        w1 = jnp.where(l1 == j, unb[j], w1)
        w2 = jnp.where(l2 == j, unb[j], w2)
    tot = w1 + w2
    eid_ref[...] = jnp.zeros_like(eid_ref)
    wt_ref[...] = jnp.zeros_like(wt_ref)
    eid_ref[0:1, :] = best * EXPERTS_PER_GROUP + l1
    eid_ref[1:2, :] = best * EXPERTS_PER_GROUP + l2
    wt_ref[0:1, :] = w1 / tot
    wt_ref[1:2, :] = w2 / tot


def _post(o_fox, o_diff, o_mla, gates, x, wf, wd, wm, wo, lg, lb, wr, rb):
    s = x.shape[0]
    tm = POST_TM
    row = lambda w: pl.BlockSpec((tm, w), lambda i: (i, 0))
    full = lambda a: pl.BlockSpec(a.shape, lambda i: (0,) * a.ndim, pipeline_mode=pl.Buffered(1))
    return pl.pallas_call(
        _post_kernel,
        grid=(s // tm,),
        in_specs=[row(512), row(512), row(512),
                  pl.BlockSpec((tm, D_MODEL), lambda i: (i, 0)),
                  pl.BlockSpec((tm, D_MODEL), lambda i: (i, 1)),
                  pl.BlockSpec((tm, D_MODEL), lambda i: (i, 2)),
                  row(D_MODEL), full(wf), full(wd), full(wm), full(wo), full(lg), full(lb), full(wr), full(rb)],
        out_specs=[row(D_MODEL), pl.BlockSpec((tm * ROW_CHUNKS, LANES), lambda i: (i, 0)),
                   pl.BlockSpec((8, tm), lambda i: (0, i)), pl.BlockSpec((8, tm), lambda i: (0, i))],
        out_shape=[jax.ShapeDtypeStruct((s, D_MODEL), F32),
                   jax.ShapeDtypeStruct((s * ROW_CHUNKS, LANES), F32),
                   jax.ShapeDtypeStruct((8, s), jnp.int32), jax.ShapeDtypeStruct((8, s), F32)],
        compiler_params=_cparams(1),
        name="post",
    )(o_fox, o_diff, o_mla, gates, gates, gates, x, wf, wd, wm, wo, lg, lb, wr, rb)


def _rank_kernel(eid_ref, us_ref, rank_ref, cnt_ref, carry_ref):
    @pl.when(pl.program_id(0) == 0)
    def _():
        carry_ref[...] = jnp.zeros_like(carry_ref)

    e0 = eid_ref[0:1, :]
    e1 = eid_ref[1:2, :]
    eio = lax.broadcasted_iota(jnp.int32, (N_EXPERTS, RANK_T), 0)
    hit0 = eio == e0
    hit1 = eio == e1
    onehot = jnp.where(hit0 | hit1, 1.0, 0.0)
    before = _dot(onehot.astype(BF16), us_ref[...]) + carry_ref[:, :1]
    rank_ref[...] = jnp.zeros_like(rank_ref)
    rank_ref[0:1, :] = jnp.sum(jnp.where(hit0, before, 0.0), axis=0, keepdims=True).astype(jnp.int32)
    rank_ref[1:2, :] = jnp.sum(jnp.where(hit1, before, 0.0), axis=0, keepdims=True).astype(jnp.int32)
    carry_ref[...] = carry_ref[...] + jnp.sum(onehot, axis=1, keepdims=True)
    cnt_ref[...] = carry_ref[...]


def _rank(eid, us):
    s = eid.shape[1]
    return pl.pallas_call(
        _rank_kernel,
        grid=(s // RANK_T,),
        in_specs=[pl.BlockSpec((8, RANK_T), lambda i: (0, i)), pl.BlockSpec(us.shape, lambda i: (0, 0))],
        out_specs=[pl.BlockSpec((8, RANK_T), lambda i: (0, i)),
                   pl.BlockSpec((N_EXPERTS, LANES), lambda i: (0, 0))],
        out_shape=[jax.ShapeDtypeStruct((8, s), jnp.int32), jax.ShapeDtypeStruct((N_EXPERTS, LANES), F32)],
        scratch_shapes=[pltpu.VMEM((N_EXPERTS, LANES), F32)],
        compiler_params=_cparams(1),
        name="rank",
    )(eid, us)


def _dest_kernel(cnt_ref, eid_ref, rank_ref, lt_ref, dest_ref, blk_ref):
    nb = jnp.floor((cnt_ref[...] + (MOE_R - 1)) * (1.0 / MOE_R))
    incl = _dot(lt_ref[...], nb.astype(BF16))
    start = (incl - nb)[:, :1] * MOE_R
    eio = lax.broadcasted_iota(jnp.int32, (N_EXPERTS, DEST_T), 0)
    dest_ref[...] = jnp.zeros_like(dest_ref)
    for k in range(TOP_K):
        base = jnp.sum(jnp.where(eio == eid_ref[k:k + 1, :], start, 0.0), axis=0, keepdims=True)
        dest_ref[k:k + 1, :] = base.astype(jnp.int32) + rank_ref[k:k + 1, :]
    bid = lax.broadcasted_iota(jnp.int32, (N_EXPERTS, BLK_LANES), 1).astype(F32)
    ended = jnp.sum(jnp.where(incl[:, :1] <= bid, 1.0, 0.0), axis=0, keepdims=True)
    blk_ref[...] = jnp.zeros_like(blk_ref)
    blk_ref[0:1, :] = jnp.minimum(ended, N_EXPERTS - 1.0).astype(jnp.int32)
    blk_ref[1:2, :] = jnp.broadcast_to(incl[N_EXPERTS - 1:N_EXPERTS, :1], (1, BLK_LANES)).astype(jnp.int32)


def _dest(cnt, eid, rank, lt):
    s = eid.shape[1]
    return pl.pallas_call(
        _dest_kernel,
        grid=(s // DEST_T,),
        in_specs=[pl.BlockSpec(cnt.shape, lambda i: (0, 0)),
                  pl.BlockSpec((8, DEST_T), lambda i: (0, i)), pl.BlockSpec((8, DEST_T), lambda i: (0, i)),
                  pl.BlockSpec(lt.shape, lambda i: (0, 0))],
        out_specs=[pl.BlockSpec((8, DEST_T), lambda i: (0, i)), pl.BlockSpec((8, BLK_LANES), lambda i: (0, 0))],
        out_shape=[jax.ShapeDtypeStruct((8, s), jnp.int32), jax.ShapeDtypeStruct((8, BLK_LANES), jnp.int32)],
        compiler_params=_cparams(1),
        name="dest",
    )(cnt, eid, rank, lt)


def _row_copy(src_ref, src_row, dst_ref, dst_row, sem):
    def rows(row):
        start = row * ROW_CHUNKS
        return pl.ds(start if isinstance(row, int) else pl.multiple_of(start, ROW_CHUNKS), ROW_CHUNKS)

    return pltpu.make_async_copy(src_ref.at[rows(src_row)], dst_ref.at[rows(dst_row)], sem)


def _dispatch_kernel(dest_ref, h_ref, xs_in_ref, xs_ref, sem):
    del xs_in_ref
    base = pl.program_id(0) * DISP_T

    def issue(r, carry):
        tok = base + r
        for k in range(TOP_K):
            _row_copy(h_ref, tok, xs_ref, dest_ref[k * SEQ + tok], sem).start()
        return carry

    lax.fori_loop(0, DISP_T, issue, 0)

    def drain(r, carry):
        _row_copy(h_ref, 0, xs_ref, 0, sem).wait()
        return carry

    lax.fori_loop(0, DISP_T * TOP_K, drain, 0)


def _dispatch(dest_flat, h16, xs_zero):
    return pl.pallas_call(
        _dispatch_kernel,
        grid_spec=pltpu.PrefetchScalarGridSpec(
            num_scalar_prefetch=1, grid=(SEQ // DISP_T,),
            in_specs=[pl.BlockSpec(memory_space=pl.ANY), pl.BlockSpec(memory_space=pl.ANY)],
            out_specs=pl.BlockSpec(memory_space=pl.ANY),
            scratch_shapes=[pltpu.SemaphoreType.DMA(())]),
        out_shape=jax.ShapeDtypeStruct(xs_zero.shape, xs_zero.dtype),
        input_output_aliases={2: 0},
        compiler_params=_cparams(1),
        name="dispatch",
    )(dest_flat, h16, xs_zero)


def _moe_kernel(be_ref, nu_ref, xs_ref, wgu_ref, wout_ref, y_ref, lhs_ref):
    del be_ref
    used = pl.program_id(0) < nu_ref[0]

    @pl.when(jnp.logical_not(used))
    def _():
        y_ref[...] = jnp.zeros_like(y_ref)

    @pl.when(used)
    def _():
        for c in range(ROW_CHUNKS):
            lhs_ref[:, c * LANES:(c + 1) * LANES] = xs_ref[pl.ds(c, MOE_R, stride=ROW_CHUNKS), :].astype(BF16)
        gu = _dot(lhs_ref[...], wgu_ref[...])
        hmid = (jax.nn.silu(gu[:, :D_FF_EXPERT]) * gu[:, D_FF_EXPERT:]).astype(BF16)
        y = _dot(hmid, wout_ref[...])
        for c in range(ROW_CHUNKS):
            y_ref[pl.ds(c, MOE_R, stride=ROW_CHUNKS), :] = y[:, c * LANES:(c + 1) * LANES]


def _moe(blk_e, n_used, xs, wgu, wout):
    last = lambda b, nu: jnp.minimum(b, nu[0] - 1)
    rows = pl.BlockSpec((MOE_R * ROW_CHUNKS, LANES), lambda b, be, nu: (last(b, nu), 0))
    return pl.pallas_call(
        _moe_kernel,
        grid_spec=pltpu.PrefetchScalarGridSpec(
            num_scalar_prefetch=2, grid=(MOE_NB,),
            in_specs=[rows,
                      pl.BlockSpec((None,) + wgu.shape[1:], lambda b, be, nu: (be[last(b, nu)], 0, 0)),
                      pl.BlockSpec((None,) + wout.shape[1:], lambda b, be, nu: (be[last(b, nu)], 0, 0))],
            out_specs=pl.BlockSpec((MOE_R * ROW_CHUNKS, LANES), lambda b, be, nu: (b, 0)),
            scratch_shapes=[pltpu.VMEM((MOE_R, D_MODEL), BF16)]),
        out_shape=jax.ShapeDtypeStruct(xs.shape, F32),
        compiler_params=_cparams(1),
        name="moe",
    )(blk_e, n_used, xs, wgu, wout)


def _combine_kernel(dest_ref, y_ref, x1_ref, wt_ref, g_ref, b_ref, o_ref, buf_ref, sem):
    base = pl.program_id(0) * COMB_T

    def issue(r, carry):
        for k in range(TOP_K):
            _row_copy(y_ref, dest_ref[k * SEQ + base + r], buf_ref.at[k], r, sem).start()
        return carry

    lax.fori_loop(0, COMB_T, issue, 0)

    def drain(r, carry):
        _row_copy(y_ref, 0, buf_ref.at[0], 0, sem).wait()
        return carry

    lax.fori_loop(0, COMB_T * TOP_K, drain, 0)

    wcol = wt_ref[...].T
    w0 = wcol[:, 0:1]
    w1 = wcol[:, 1:2]
    ffn = jnp.concatenate(
        [w0 * buf_ref[0, pl.ds(c, COMB_T, stride=ROW_CHUNKS), :] + w1 * buf_ref[1, pl.ds(c, COMB_T, stride=ROW_CHUNKS), :]
         for c in range(ROW_CHUNKS)], axis=1)
    o_ref[...] = _layer_norm(DEEPNORM_ALPHA * x1_ref[...] + ffn, g_ref[...], b_ref[...])


def _combine(dest_flat, y16, x1, wt, g, b):
    tm = COMB_T
    return pl.pallas_call(
        _combine_kernel,
        grid_spec=pltpu.PrefetchScalarGridSpec(
            num_scalar_prefetch=1, grid=(SEQ // tm,),
            in_specs=[pl.BlockSpec(memory_space=pl.ANY),
                      pl.BlockSpec((tm, D_MODEL), lambda i, d: (i, 0)),
                      pl.BlockSpec((8, tm), lambda i, d: (0, i)),
                      pl.BlockSpec(g.shape, lambda i, d: (0, 0)), pl.BlockSpec(b.shape, lambda i, d: (0, 0))],
            out_specs=pl.BlockSpec((tm, D_MODEL), lambda i, d: (i, 0)),
            scratch_shapes=[pltpu.VMEM((TOP_K, tm * ROW_CHUNKS, LANES), F32), pltpu.SemaphoreType.DMA(())]),
        out_shape=jax.ShapeDtypeStruct(x1.shape, F32),
        compiler_params=_cparams(1),
        name="combine",
    )(dest_flat, y16, x1, wt, g, b)


def _rot_matrix(width, groups, half):
    r = np.zeros((width, width), np.float32)
    for o in groups:
        for j in range(half):
            r[o + j + half, o + j] = -1.0
            r[o + j, o + half + j] = 1.0
    return jnp.asarray(r, BF16)


def _rope_tables():
    pos = jnp.arange(SEQ, dtype=F32)

    def cs(dim):
        inv_freq = ROPE_THETA ** (-jnp.arange(0, dim, 2, dtype=F32) / dim)
        ang = pos[:, None] * inv_freq[None, :]
        return jnp.cos(ang), jnp.sin(ang)

    cp, sp = cs(DIFF_ROT_DIM)
    one = jnp.ones((SEQ, DIFF_HEAD_DIM - DIFF_ROT_DIM), F32)
    cd = jnp.tile(jnp.concatenate([cp, cp, one], axis=1), (1, 2 * DIFF_HEADS))
    sd = jnp.tile(jnp.concatenate([sp, sp, 0.0 * one], axis=1), (1, 2 * DIFF_HEADS))
    cm, sm = cs(MLA_ROPE_DIM)
    scale = MLA_QK_DIM ** -0.5
    ones_n = jnp.ones((SEQ, MLA_NOPE_DIM), F32)
    pad = jnp.zeros((SEQ, MLA_QK_PAD - MLA_QK_DIM), F32)
    cmq = jnp.tile(jnp.concatenate([ones_n, cm, cm, pad], axis=1) * scale, (1, MLA_HEADS))
    smq = jnp.tile(jnp.concatenate([0.0 * ones_n, sm, sm, pad], axis=1) * scale, (1, MLA_HEADS))
    ckp = jnp.concatenate([cm, cm, pad], axis=1)
    skp = jnp.concatenate([sm, sm, pad], axis=1)
    return cd, sd, cmq, smq, ckp, skp


def _constants():
    rd = _rot_matrix(512, [c * DIFF_HEAD_DIM for c in range(2 * DIFF_HEADS)], DIFF_ROT_DIM // 2)
    rmq = _rot_matrix(MLA_HEADS * MLA_QK_PAD, [h * MLA_QK_PAD + MLA_NOPE_DIM for h in range(MLA_HEADS)],
                      MLA_ROPE_DIM // 2)
    rkp = _rot_matrix(LANES, [0], MLA_ROPE_DIM // 2)
    tri = np.arange(CUM_T)
    u_incl = jnp.asarray(tri[:, None] <= tri[None, :], BF16)
    tri = np.arange(RANK_T)
    u_strict = jnp.asarray(tri[:, None] < tri[None, :], BF16)
    tri = np.arange(N_EXPERTS)
    l_incl = jnp.asarray(tri[:, None] >= tri[None, :], BF16)
    return (rd, rmq, rkp), u_incl, u_strict, l_incl


def kernel(x, w_in, b_fox_f, b_gate, diff_lambda, g_diff, g_mla_q, g_mla_kv, w_mla_uq, w_mla_ukv, w_fox_up, w_diff_up, w_mla_up, w_o, ln1_g, ln1_b, ln2_g, ln2_b, w_router, router_bias, w_exp_in, w_exp_out):
    s = x.shape[1]
    assert x.shape == (1, SEQ, D_MODEL)
    xf = x.reshape(s, D_MODEL)
    tabs = _rope_tables()
    rmats, u_incl, u_strict, l_incl = _constants()

    perm = np.array([g * EXPERTS_PER_GROUP + j for j in range(EXPERTS_PER_GROUP) for g in range(N_GROUPS)])
    wr = w_router.T[perm]
    rb = router_bias[perm].reshape(N_EXPERTS, 1)

    o_q = 0
    o_f = 3 * 512
    o_dq = o_f + FOX_HEADS
    o_cq = o_dq + 3 * 512
    o_kpe = o_cq + 2 * 512
    o_gate = o_kpe + MLA_ROPE_DIM

    for i in range(DEPTH):
        w = w_in[i]
        w_a = jnp.concatenate([w[:, o_q:o_f], w[:, o_dq:o_cq], w[:, o_cq:o_kpe]], axis=1).astype(BF16)
        w_s = jnp.concatenate([w[:, o_kpe:o_gate], w[:, o_f:o_dq],
                               jnp.zeros((D_MODEL, LANES - MLA_ROPE_DIM - FOX_HEADS), F32)], axis=1).astype(BF16)
        w_g = w[:, o_gate:].astype(BF16)
        zero_a = jnp.zeros((1, w_a.shape[1]), F32)
        zero_s = jnp.zeros((1, LANES), F32)
        pa = _linear(xf, w_a, zero_a, None, BF16, LIN_TN, "inproj_attn")
        ps = _linear(xf, w_s, zero_s, None, F32, LANES, "inproj_small")
        gates = _linear(xf, w_g, b_gate[i].reshape(1, -1), "sigmoid", BF16, LIN_TN, "inproj_gate")

        uq = w_mla_uq[i].reshape(MLA_Q_RANK, MLA_HEADS, MLA_QK_DIM)
        wuq = jnp.concatenate([uq, jnp.zeros((MLA_Q_RANK, MLA_HEADS, MLA_QK_PAD - MLA_QK_DIM), F32)], axis=2)
        wuq = wuq.reshape(MLA_Q_RANK, MLA_HEADS * MLA_QK_PAD).astype(BF16)
        ukv = w_mla_ukv[i].reshape(MLA_KV_RANK, MLA_HEADS, MLA_NOPE_DIM + MLA_V_DIM)
        wukv = jnp.concatenate([ukv[:, :, :MLA_NOPE_DIM].reshape(MLA_KV_RANK, -1),
                                ukv[:, :, MLA_NOPE_DIM:].reshape(MLA_KV_RANK, -1)], axis=1).astype(BF16)
        bf = jnp.zeros((1, LANES), F32).at[0, MLA_ROPE_DIM:MLA_ROPE_DIM + FOX_HEADS].set(b_fox_f[i])

        dq, dk, mq, mk, mv, lft = _prep(pa, ps, tabs, g_mla_q[i].reshape(1, -1), g_mla_kv[i].reshape(1, -1),
                                        wuq, wukv, rmats, bf)
        cum = _cumsum(lft, u_incl)
        cum_row = cum[:FOX_HEADS].reshape(FOX_HEADS, 1, s)

        lam_init = 0.8 - 0.6 * math.exp(-0.3 * i)
        o_fox = _flash("fox", pa, 0, pa, 4, pa, 8, FOX_HEAD_DIM, FOX_HEADS, extra=(cum_row,))
        o_diff = _flash("diff", dq, 0, dk, 0, pa, 20, 2 * DIFF_HEAD_DIM, DIFF_HEADS,
                        extra=(diff_lambda[i], g_diff[i].reshape(1, -1)), lam_init=lam_init)
        o_mla = _flash("mla", mq, 0, mk, 0, mv, 0, MLA_QK_PAD, MLA_HEADS)

        x1, h16, eid, wt = _post(o_fox, o_diff, o_mla, gates, xf,
                                 w_fox_up[i].astype(BF16), w_diff_up[i].astype(BF16), w_mla_up[i].astype(BF16),
                                 w_o[i].astype(BF16), ln1_g[i].reshape(1, -1), ln1_b[i].reshape(1, -1), wr, rb)

        rank, cnt = _rank(eid, u_strict)
        dest, blk = _dest(cnt, eid, rank, l_incl)
        dest_flat = dest[:TOP_K].reshape(TOP_K * s)
        xs = _dispatch(dest_flat, h16, jnp.zeros((MOE_ROWS * ROW_CHUNKS, LANES), F32))
        y16 = _moe(blk[0], blk[1, :1], xs, w_exp_in[i].astype(BF16), w_exp_out[i].astype(BF16))
        xf = _combine(dest_flat, y16, x1, wt, ln2_g[i].reshape(1, -1), ln2_b[i].reshape(1, -1))
    return xf.reshape(1, s, D_MODEL)
```

```python
import functools
import math

import numpy as np
import jax
import jax.numpy as jnp
from jax import lax
from jax.experimental import pallas as pl
from jax.experimental.pallas import tpu as pltpu

F32 = jnp.float32
BF16 = jnp.bfloat16

D_MODEL = 2048
SEQ = 16384
DEPTH = 2
ROPE_THETA = 500000.0
NORM_EPS = 1e-5
FOX_HEADS = 4
FOX_HEAD_DIM = 128
DIFF_HEADS = 4
DIFF_HEAD_DIM = 64
DIFF_V_DIM = 128
DIFF_ROT_DIM = 16
MLA_HEADS = 4
MLA_Q_RANK = 512
MLA_KV_RANK = 512
MLA_NOPE_DIM = 128
MLA_ROPE_DIM = 64
MLA_V_DIM = 128
MLA_QK_DIM = MLA_NOPE_DIM + MLA_ROPE_DIM
N_EXPERTS = 32
N_GROUPS = 8
EXPERTS_PER_GROUP = 4
TOP_K = 2
D_FF_EXPERT = 1408
DEEPNORM_ALPHA = (2 * DEPTH) ** 0.25

LANES = 128
ROW_CHUNKS = D_MODEL // LANES
MLA_QK_PAD = 256
FOX_QK_PAD = 256
BF16_ROWS = 16
VT_ROWS = LANES + BF16_ROWS
LOG2E = math.log2(math.e)
VMEM_LIMIT = 56 * 1024 * 1024

LIN_TM = 1024
LIN_TN = 1024
PREP_TM = 512
CUM_T = 512
ATT_T = 512
POST_TM = 256
RANK_T = 512
DEST_T = 2048
DISP_T = 512
MOE_R = 256
MOE_NB = (SEQ * TOP_K + N_EXPERTS * (MOE_R - 1) + MOE_R - 1) // MOE_R
MOE_ROWS = MOE_NB * MOE_R
BLK_LANES = 256
COMB_T = 256
NEG = -1e30


def _cparams(n_axes, **kw):
    return pltpu.CompilerParams(dimension_semantics=("arbitrary",) * n_axes,
                                vmem_limit_bytes=VMEM_LIMIT, **kw)


def _dot(a, b):
    return jnp.dot(a, b, preferred_element_type=F32)


def _dot_nt(a, b):
    return lax.dot_general(a, b, (((1,), (1,)), ((), ())), preferred_element_type=F32)


def _split3(x):
    a = x.astype(BF16)
    r = x - a.astype(F32)
    b = r.astype(BF16)
    c = (r - b.astype(F32)).astype(BF16)
    return a, b, c


def _linear_kernel(x_ref, w_ref, b_ref, o_ref, xb_ref, *, act):
    @pl.when(pl.program_id(1) == 0)
    def _():
        xb_ref[...] = x_ref[...].astype(BF16)

    acc = _dot(xb_ref[...], w_ref[...]) + b_ref[...]
    if act == "sigmoid":
        acc = jax.nn.sigmoid(acc)
    o_ref[...] = acc.astype(o_ref.dtype)


def _linear(x, w, layer, b, act, out_dtype, tn, name):
    m, k = x.shape
    n = w.shape[2]
    return pl.pallas_call(
        functools.partial(_linear_kernel, act=act),
        grid=(m // LIN_TM, n // tn),
        in_specs=[pl.BlockSpec((LIN_TM, k), lambda i, j: (i, 0)),
                  pl.BlockSpec((None, k, tn), lambda i, j: (layer, 0, j)),
                  pl.BlockSpec((1, tn), lambda i, j: (0, j))],
        out_specs=pl.BlockSpec((LIN_TM, tn), lambda i, j: (i, j)),
        out_shape=jax.ShapeDtypeStruct((m, n), out_dtype),
        scratch_shapes=[pltpu.VMEM((LIN_TM, k), BF16)],
        compiler_params=_cparams(2),
        name=name,
    )(x, w, b)


O_FOX_F = 3 * 512
O_DIFF = O_FOX_F + FOX_HEADS
O_MLA = O_DIFF + 3 * 512
O_KPE = O_MLA + 2 * 512
O_GATE = O_KPE + MLA_ROPE_DIM
IN_WIDTH = O_GATE + 3 * D_MODEL
W_ATTN = 4096
REPACK_TK = 256


def _repack_kernel(w_ref, wa_ref, ws_ref, wg_ref):
    wa_ref[:, 0:O_FOX_F] = w_ref[:, 0:O_FOX_F].astype(BF16)
    wa_ref[:, O_FOX_F:W_ATTN] = w_ref[:, O_DIFF:O_KPE].astype(BF16)
    ws_ref[...] = jnp.zeros_like(ws_ref)
    ws_ref[:, 0:MLA_ROPE_DIM] = w_ref[:, O_KPE:O_GATE].astype(BF16)
    ws_ref[:, MLA_ROPE_DIM:MLA_ROPE_DIM + FOX_HEADS] = w_ref[:, O_FOX_F:O_DIFF].astype(BF16)
    wg_ref[...] = w_ref[:, O_GATE:IN_WIDTH].astype(BF16)


def _repack(w_in):
    k = w_in.shape[1]
    blk = lambda n: pl.BlockSpec((None, REPACK_TK, n), lambda l, i: (l, i, 0))
    return pl.pallas_call(
        _repack_kernel,
        grid=(DEPTH, k // REPACK_TK),
        in_specs=[blk(IN_WIDTH)],
        out_specs=[blk(W_ATTN), blk(LANES), blk(3 * D_MODEL)],
        out_shape=[jax.ShapeDtypeStruct((DEPTH, k, W_ATTN), BF16), jax.ShapeDtypeStruct((DEPTH, k, LANES), BF16),
                   jax.ShapeDtypeStruct((DEPTH, k, 3 * D_MODEL), BF16)],
        compiler_params=_cparams(2),
        name="repack",
    )(w_in)


def _rms(xf, g):
    ms = jnp.mean(jnp.square(xf), axis=-1, keepdims=True)
    return xf * lax.rsqrt(ms + NORM_EPS) * g


def _rope(x_f32, x_b16, r_ref, c, s):
    return x_f32 * c + _dot(x_b16, r_ref[...]) * s


def _store_vt(vt_ref, v, heads):
    vt = v.T.astype(BF16)
    ones = jnp.ones((BF16_ROWS, v.shape[0]), BF16)
    for h in range(heads):
        vt_ref[h * VT_ROWS:h * VT_ROWS + LANES, :] = vt[h * LANES:(h + 1) * LANES]
        vt_ref[h * VT_ROWS + LANES:(h + 1) * VT_ROWS, :] = ones


def _prep_kernel(dq_ref, dk_ref, dv_ref, cq_ref, ckv_ref, ps_ref, cd_ref, sd_ref, cmq_ref, smq_ref, ckp_ref, skp_ref,
                 gq_ref, gkv_ref, wuq_ref, wukv_ref, rd_ref, rmq_ref, rkp_ref, bf_ref,
                 dqo_ref, dko_ref, dvt_ref, mq_ref, mk_ref, mvt_ref, lft_ref):
    cd = cd_ref[...]
    sd = sd_ref[...]
    dq = dq_ref[...]
    dk = dk_ref[...]
    dqo_ref[...] = (_rope(dq.astype(F32), dq, rd_ref, cd, sd) * (DIFF_HEAD_DIM ** -0.5 * LOG2E)).astype(BF16)
    dko_ref[...] = _rope(dk.astype(F32), dk, rd_ref, cd, sd).astype(BF16)
    _store_vt(dvt_ref, dv_ref[...].astype(F32), DIFF_HEADS)

    cqn = _rms(cq_ref[...].astype(F32), gq_ref[...]).astype(BF16)
    q = _dot(cqn, wuq_ref[...])
    mq_ref[...] = _rope(q, q.astype(BF16), rmq_ref, cmq_ref[...], smq_ref[...]).astype(BF16)

    ckvn = _rms(ckv_ref[...].astype(F32), gkv_ref[...]).astype(BF16)
    kv = _dot(ckvn, wukv_ref[...])
    ps = ps_ref[...]
    kpe = _rope(ps, ps.astype(BF16), rkp_ref, ckp_ref[...], skp_ref[...]).astype(BF16)
    for h in range(MLA_HEADS):
        mk_ref[:, h * MLA_QK_PAD:h * MLA_QK_PAD + LANES] = kv[:, h * LANES:(h + 1) * LANES].astype(BF16)
        mk_ref[:, h * MLA_QK_PAD + LANES:(h + 1) * MLA_QK_PAD] = kpe
    _store_vt(mvt_ref, kv[:, MLA_HEADS * LANES:], MLA_HEADS)

    z = ps + bf_ref[...]
    lf = jnp.minimum(z, 0.0) - jnp.log1p(jnp.exp(-jnp.abs(z)))
    lft_ref[...] = lf.T[MLA_ROPE_DIM:MLA_ROPE_DIM + 8, :]


def _prep(pa, ps, tabs, gq, gkv, wuq, wukv, rmats, bf):
    s = pa.shape[0]
    tm = PREP_TM
    row = lambda w: pl.BlockSpec((tm, w), lambda i: (i, 0))
    col = lambda w, c: pl.BlockSpec((tm, w), lambda i: (i, c))
    tr = lambda r: pl.BlockSpec((r, tm), lambda i: (0, i))
    full = lambda a: pl.BlockSpec(a.shape, lambda i: (0,) * a.ndim)
    cd, sd, cmq, smq, ckp, skp = tabs
    rd, rmq, rkp = rmats
    return pl.pallas_call(
        _prep_kernel,
        grid=(s // tm,),
        in_specs=[col(512, 3), col(512, 4), col(512, 5), col(512, 6), col(512, 7), row(LANES),
                  row(512), row(512), row(1024), row(1024), row(LANES), row(LANES),
                  full(gq), full(gkv), full(wuq), full(wukv), full(rd), full(rmq), full(rkp), full(bf)],
        out_specs=[row(512), row(512), tr(4 * VT_ROWS), row(1024), row(1024), tr(4 * VT_ROWS), tr(8)],
        out_shape=[jax.ShapeDtypeStruct((s, 512), BF16), jax.ShapeDtypeStruct((s, 512), BF16),
                   jax.ShapeDtypeStruct((4 * VT_ROWS, s), BF16),
                   jax.ShapeDtypeStruct((s, 1024), BF16), jax.ShapeDtypeStruct((s, 1024), BF16),
                   jax.ShapeDtypeStruct((4 * VT_ROWS, s), BF16), jax.ShapeDtypeStruct((8, s), F32)],
        compiler_params=_cparams(1),
        name="prep",
    )(pa, pa, pa, pa, pa, ps, cd, sd, cmq, smq, ckp, skp, gq, gkv, wuq, wukv, rd, rmq, rkp, bf)


def _cumsum_kernel(x_ref, u_ref, o_ref, carry_ref):
    @pl.when(pl.program_id(0) == 0)
    def _():
        carry_ref[...] = jnp.zeros_like(carry_ref)

    a, b, c = _split3(x_ref[...])
    u = u_ref[...]
    cs = _dot(a, u) + _dot(b, u) + _dot(c, u) + carry_ref[:, :1]
    o_ref[...] = cs
    carry_ref[...] = jnp.broadcast_to(cs[:, CUM_T - 1:CUM_T], carry_ref.shape)


def _cumsum(x, u):
    s = x.shape[1]
    return pl.pallas_call(
        _cumsum_kernel,
        grid=(s // CUM_T,),
        in_specs=[pl.BlockSpec((8, CUM_T), lambda i: (0, i)), pl.BlockSpec(u.shape, lambda i: (0, 0))],
        out_specs=pl.BlockSpec((8, CUM_T), lambda i: (0, i)),
        out_shape=jax.ShapeDtypeStruct((8, s), F32),
        scratch_shapes=[pltpu.VMEM((8, LANES), F32)],
        compiler_params=_cparams(1),
        name="cumsum",
    )(x, u)


N_BIAS = 3


def _foxprep_kernel(q_ref, k_ref, v_ref, cum_ref, qo_ref, ko_ref, vt_ref):
    cum = cum_ref[...]
    beta = (cum[:, 0:1] - cum) * LOG2E
    pieces = [p.astype(F32).T for p in _split3(beta)]
    lane = lax.broadcasted_iota(jnp.int32, (ATT_T, LANES), 1)
    ones = jnp.where(lane < N_BIAS, 1.0, 0.0).astype(BF16)
    q = q_ref[...].astype(F32) * (FOX_HEAD_DIM ** -0.5 * LOG2E)
    for h in range(FOX_HEADS):
        e = jnp.zeros((ATT_T, LANES), F32)
        for n, p in enumerate(pieces):
            e = jnp.where(lane == n, p[:, h:h + 1], e)
        ko_ref[:, h * FOX_QK_PAD:h * FOX_QK_PAD + LANES] = k_ref[:, h * LANES:(h + 1) * LANES]
        ko_ref[:, h * FOX_QK_PAD + LANES:(h + 1) * FOX_QK_PAD] = e.astype(BF16)
        qo_ref[:, h * FOX_QK_PAD:h * FOX_QK_PAD + LANES] = q[:, h * LANES:(h + 1) * LANES].astype(BF16)
        qo_ref[:, h * FOX_QK_PAD + LANES:(h + 1) * FOX_QK_PAD] = ones
    _store_vt(vt_ref, v_ref[...].astype(F32), FOX_HEADS)


def _foxprep(pa, cum):
    s = pa.shape[0]
    t = ATT_T
    w = FOX_HEADS * FOX_QK_PAD
    return pl.pallas_call(
        _foxprep_kernel,
        grid=(s // t,),
        in_specs=[pl.BlockSpec((t, 512), lambda i: (i, 0)), pl.BlockSpec((t, 512), lambda i: (i, 1)),
                  pl.BlockSpec((t, 512), lambda i: (i, 2)), pl.BlockSpec((8, t), lambda i: (0, i))],
        out_specs=[pl.BlockSpec((t, w), lambda i: (i, 0)), pl.BlockSpec((t, w), lambda i: (i, 0)),
                   pl.BlockSpec((FOX_HEADS * VT_ROWS, t), lambda i: (0, i))],
        out_shape=[jax.ShapeDtypeStruct((s, w), BF16), jax.ShapeDtypeStruct((s, w), BF16),
                   jax.ShapeDtypeStruct((FOX_HEADS * VT_ROWS, s), BF16)],
        compiler_params=_cparams(1),
        name="foxprep",
    )(pa, pa, pa, cum)


def _flash_kernel(*refs, mode, lam_init):
    t = ATT_T
    if mode == "fox":
        q_ref, k_ref, vt_ref, ck_ref, o_ref, sa_ref, sb_ref, m_ref, acc_ref = refs
    elif mode == "diff":
        q_ref, k_ref, vt_ref, lam_ref, gd_ref, o_ref, sa_ref, sb_ref, m_ref, acc_ref = refs
    else:
        q_ref, k_ref, vt_ref, o_ref, sa_ref, sb_ref, m_ref, acc_ref = refs
    i = pl.program_id(1)
    off_i = pl.multiple_of(i * t, t)
    m_ref[...] = jnp.full(m_ref.shape, NEG, F32)
    acc_ref[...] = jnp.zeros_like(acc_ref)

    q = q_ref[...]
    if mode == "diff":
        lane = lax.broadcasted_iota(jnp.int32, q.shape, 1)
        zero = jnp.zeros_like(q)
        qs = [jnp.where(lane < DIFF_HEAD_DIM, q, zero), jnp.where(lane >= DIFF_HEAD_DIM, q, zero)]
    else:
        qs = [q]

    def scores(off, s_ref):
        kb = k_ref[pl.ds(off, t), :]
        for n, qn in enumerate(qs):
            s_ref[n] = _dot_nt(kb, qn)

    def consume(off, s_ref, masked):
        vtb = vt_ref[:, pl.ds(off, t)]
        for n in range(len(qs)):
            st = s_ref[n]
            if masked:
                key_id = lax.broadcasted_iota(jnp.int32, st.shape, 0)
                qry_id = lax.broadcasted_iota(jnp.int32, st.shape, 1)
                st = jnp.where(key_id <= qry_id, st, NEG)
            m = jnp.max(st, axis=0, keepdims=True)
            p = jnp.exp2((st - m).astype(BF16))
            o = _dot(vtb, p)
            if mode == "fox":
                m = m + (ck_ref[:, pl.ds(off_i, LANES)][:, :1] - ck_ref[:, pl.ds(off, LANES)][:, :1]) * LOG2E
            m_prev = m_ref[n]
            m_new = jnp.maximum(m_prev, m)
            acc_ref[n] = jnp.exp2(m_prev - m_new) * acc_ref[n] + jnp.exp2(m - m_new) * o
            m_ref[n] = m_new

    scores(0, sa_ref)

    def pair(jj, carry):
        off0 = pl.multiple_of(jj * (2 * t), 2 * t)
        off1 = pl.multiple_of(off0 + t, t)
        scores(off1, sb_ref)
        consume(off0, sa_ref, False)
        scores(pl.multiple_of(off1 + t, t), sa_ref)
        consume(off1, sb_ref, False)
        return carry

    lax.fori_loop(0, lax.shift_right_logical(i, 1), pair, 0)
    odd = (i & 1) == 1

    @pl.when(odd)
    def _():
        off_p = pl.multiple_of(off_i - t, t)
        scores(off_i, sb_ref)
        consume(off_p, sa_ref, False)
        consume(off_i, sb_ref, True)

    @pl.when(jnp.logical_not(odd))
    def _():
        consume(off_i, sa_ref, True)

    def normalised(n):
        acc = acc_ref[n]
        return acc[:LANES] / acc[LANES:LANES + 1]

    if mode == "diff":
        dl = lam_ref[...]
        lam = (jnp.exp(jnp.sum(dl[0:1] * dl[1:2], axis=1, keepdims=True))
               - jnp.exp(jnp.sum(dl[2:3] * dl[3:4], axis=1, keepdims=True)) + lam_init)
        o = _rms((normalised(0) - lam * normalised(1)).T, gd_ref[...]) * (1.0 - lam_init)
    else:
        o = normalised(0).T
    o_ref[...] = o.astype(o_ref.dtype)


def _flash(mode, q_arr, k_arr, vt_arr, dk, heads, extra=(), lam_init=0.0):
    s = q_arr.shape[0]
    t = ATT_T
    dv = LANES
    n_stream = 2 if mode == "diff" else 1
    in_specs = [pl.BlockSpec((t, dk), lambda h, i: (i, h)),
                pl.BlockSpec((s, dk), lambda h, i: (0, h)),
                pl.BlockSpec((VT_ROWS, s), lambda h, i: (h, 0))]
    if mode == "fox":
        in_specs.append(pl.BlockSpec((None, 1, s), lambda h, i: (h, 0, 0)))
    elif mode == "diff":
        in_specs += [pl.BlockSpec(extra[0].shape, lambda h, i: (0, 0)),
                     pl.BlockSpec(extra[1].shape, lambda h, i: (0, 0))]
    return pl.pallas_call(
        functools.partial(_flash_kernel, mode=mode, lam_init=lam_init),
        grid=(heads, s // t),
        in_specs=in_specs,
        out_specs=pl.BlockSpec((t, dv), lambda h, i: (i, h)),
        out_shape=jax.ShapeDtypeStruct((s, heads * dv), BF16),
        scratch_shapes=[pltpu.VMEM((n_stream, t, t), F32), pltpu.VMEM((n_stream, t, t), F32),
                        pltpu.VMEM((n_stream, 1, t), F32), pltpu.VMEM((n_stream, VT_ROWS, t), F32)],
        compiler_params=_cparams(2),
        name="flash_" + mode,
    )(q_arr, k_arr, vt_arr, *extra)


def _layer_norm(z, g, b):
    mu = jnp.mean(z, axis=-1, keepdims=True)
    zc = z - mu
    var = jnp.mean(jnp.square(zc), axis=-1, keepdims=True)
    return zc * lax.rsqrt(var + NORM_EPS) * g + b


def _first_index_of_max(vals):
    best = vals[0]
    for v in vals[1:]:
        best = jnp.maximum(best, v)
    idx = jnp.full(best.shape, len(vals) - 1, jnp.int32)
    for j in range(len(vals) - 2, -1, -1):
        idx = jnp.where(vals[j] == best, j, idx)
    return best, idx


def _post_kernel(of_ref, od_ref, om_ref, g0_ref, g1_ref, g2_ref, x_ref, wf_ref, wd_ref, wm_ref, wo_ref,
                 lg_ref, lb_ref, wr_ref, rb_ref, x1_ref, h16_ref, eid_ref, wt_ref):
    tm = POST_TM
    merged = (g0_ref[...].astype(F32) * _dot(of_ref[...], wf_ref[...])
              + g1_ref[...].astype(F32) * _dot(od_ref[...], wd_ref[...])
              + g2_ref[...].astype(F32) * _dot(om_ref[...], wm_ref[...]))
    mix = _dot(merged.astype(BF16), wo_ref[...])
    x1 = _layer_norm(DEEPNORM_ALPHA * x_ref[...] + mix, lg_ref[...], lb_ref[...])
    x1_ref[...] = x1
    for c in range(ROW_CHUNKS):
        h16_ref[pl.ds(c, tm, stride=ROW_CHUNKS), :] = x1[:, c * LANES:(c + 1) * LANES]

    logits = lax.dot_general(wr_ref[...], x1, (((1,), (1,)), ((), ())),
                             precision=lax.Precision.HIGHEST, preferred_element_type=F32)
    scores = jax.nn.sigmoid(logits)
    biased = scores + rb_ref[...]
    a = [biased[j * 8:(j + 1) * 8] for j in range(EXPERTS_PER_GROUP)]
    u = [scores[j * 8:(j + 1) * 8] for j in range(EXPERTS_PER_GROUP)]
    gs = a[0] + a[1]
    for j0 in range(EXPERTS_PER_GROUP):
        for j1 in range(j0 + 1, EXPERTS_PER_GROUP):
            gs = jnp.maximum(gs, a[j0] + a[j1])
    gmax = jnp.max(gs, axis=0, keepdims=True)
    gid = lax.broadcasted_iota(jnp.int32, gs.shape, 0)
    best = jnp.min(jnp.where(gs == gmax, gid, N_GROUPS), axis=0, keepdims=True)
    sel = gid == best
    ing = [jnp.sum(jnp.where(sel, a[j], 0.0), axis=0, keepdims=True) for j in range(EXPERTS_PER_GROUP)]
    unb = [jnp.sum(jnp.where(sel, u[j], 0.0), axis=0, keepdims=True) for j in range(EXPERTS_PER_GROUP)]
    _, l1 = _first_index_of_max(ing)
    ing2 = [jnp.where(l1 == j, -jnp.inf, ing[j]) for j in range(EXPERTS_PER_GROUP)]
    _, l2 = _first_index_of_max(ing2)
    w1 = jnp.zeros_like(unb[0])
    w2 = jnp.zeros_like(unb[0])
    for j in range(EXPERTS_PER_GROUP):
        w1 = jnp.where(l1 == j, unb[j], w1)
        w2 = jnp.where(l2 == j, unb[j], w2)
    tot = w1 + w2
    eid_ref[...] = jnp.zeros_like(eid_ref)
    wt_ref[...] = jnp.zeros_like(wt_ref)
    eid_ref[0:1, :] = best * EXPERTS_PER_GROUP + l1
    eid_ref[1:2, :] = best * EXPERTS_PER_GROUP + l2
    wt_ref[0:1, :] = w1 / tot
    wt_ref[1:2, :] = w2 / tot


def _post(o_fox, o_diff, o_mla, gates, x, wf, wd, wm, wo, lg, lb, wr, rb):
    s = x.shape[0]
    tm = POST_TM
    row = lambda w: pl.BlockSpec((tm, w), lambda i: (i, 0))
    full = lambda a: pl.BlockSpec(a.shape, lambda i: (0,) * a.ndim, pipeline_mode=pl.Buffered(1))
    return pl.pallas_call(
        _post_kernel,
        grid=(s // tm,),
        in_specs=[row(512), row(512), row(512),
                  pl.BlockSpec((tm, D_MODEL), lambda i: (i, 0)),
                  pl.BlockSpec((tm, D_MODEL), lambda i: (i, 1)),
                  pl.BlockSpec((tm, D_MODEL), lambda i: (i, 2)),
                  row(D_MODEL), full(wf), full(wd), full(wm), full(wo), full(lg), full(lb), full(wr), full(rb)],
        out_specs=[row(D_MODEL), pl.BlockSpec((tm * ROW_CHUNKS, LANES), lambda i: (i, 0)),
                   pl.BlockSpec((8, tm), lambda i: (0, i)), pl.BlockSpec((8, tm), lambda i: (0, i))],
        out_shape=[jax.ShapeDtypeStruct((s, D_MODEL), F32),
                   jax.ShapeDtypeStruct((s * ROW_CHUNKS, LANES), F32),
                   jax.ShapeDtypeStruct((8, s), jnp.int32), jax.ShapeDtypeStruct((8, s), F32)],
        compiler_params=_cparams(1),
        name="post",
    )(o_fox, o_diff, o_mla, gates, gates, gates, x, wf, wd, wm, wo, lg, lb, wr, rb)


def _rank_kernel(eid_ref, us_ref, rank_ref, cnt_ref, carry_ref):
    @pl.when(pl.program_id(0) == 0)
    def _():
        carry_ref[...] = jnp.zeros_like(carry_ref)

    e0 = eid_ref[0:1, :]
    e1 = eid_ref[1:2, :]
    eio = lax.broadcasted_iota(jnp.int32, (N_EXPERTS, RANK_T), 0)
    hit0 = eio == e0
    hit1 = eio == e1
    onehot = jnp.where(hit0 | hit1, 1.0, 0.0)
    before = _dot(onehot.astype(BF16), us_ref[...]) + carry_ref[:, :1]
    rank_ref[...] = jnp.zeros_like(rank_ref)
    rank_ref[0:1, :] = jnp.sum(jnp.where(hit0, before, 0.0), axis=0, keepdims=True).astype(jnp.int32)
    rank_ref[1:2, :] = jnp.sum(jnp.where(hit1, before, 0.0), axis=0, keepdims=True).astype(jnp.int32)
    carry_ref[...] = carry_ref[...] + jnp.sum(onehot, axis=1, keepdims=True)
    cnt_ref[...] = carry_ref[...]


def _rank(eid, us):
    s = eid.shape[1]
    return pl.pallas_call(
        _rank_kernel,
        grid=(s // RANK_T,),
        in_specs=[pl.BlockSpec((8, RANK_T), lambda i: (0, i)), pl.BlockSpec(us.shape, lambda i: (0, 0))],
        out_specs=[pl.BlockSpec((8, RANK_T), lambda i: (0, i)),
                   pl.BlockSpec((N_EXPERTS, LANES), lambda i: (0, 0))],
        out_shape=[jax.ShapeDtypeStruct((8, s), jnp.int32), jax.ShapeDtypeStruct((N_EXPERTS, LANES), F32)],
        scratch_shapes=[pltpu.VMEM((N_EXPERTS, LANES), F32)],
        compiler_params=_cparams(1),
        name="rank",
    )(eid, us)


def _dest_kernel(cnt_ref, eid_ref, rank_ref, lt_ref, dest_ref, blk_ref):
    nb = jnp.floor((cnt_ref[...] + (MOE_R - 1)) * (1.0 / MOE_R))
    incl = _dot(lt_ref[...], nb.astype(BF16))
    start = (incl - nb)[:, :1] * MOE_R
    eio = lax.broadcasted_iota(jnp.int32, (N_EXPERTS, DEST_T), 0)
    dest_ref[...] = jnp.zeros_like(dest_ref)
    for k in range(TOP_K):
        base = jnp.sum(jnp.where(eio == eid_ref[k:k + 1, :], start, 0.0), axis=0, keepdims=True)
        dest_ref[k:k + 1, :] = base.astype(jnp.int32) + rank_ref[k:k + 1, :]
    bid = lax.broadcasted_iota(jnp.int32, (N_EXPERTS, BLK_LANES), 1).astype(F32)
    ended = jnp.sum(jnp.where(incl[:, :1] <= bid, 1.0, 0.0), axis=0, keepdims=True)
    blk_ref[...] = jnp.zeros_like(blk_ref)
    blk_ref[0:1, :] = jnp.minimum(ended, N_EXPERTS - 1.0).astype(jnp.int32)
    blk_ref[1:2, :] = jnp.broadcast_to(incl[N_EXPERTS - 1:N_EXPERTS, :1], (1, BLK_LANES)).astype(jnp.int32)


def _dest(cnt, eid, rank, lt):
    s = eid.shape[1]
    return pl.pallas_call(
        _dest_kernel,
        grid=(s // DEST_T,),
        in_specs=[pl.BlockSpec(cnt.shape, lambda i: (0, 0)),
                  pl.BlockSpec((8, DEST_T), lambda i: (0, i)), pl.BlockSpec((8, DEST_T), lambda i: (0, i)),
                  pl.BlockSpec(lt.shape, lambda i: (0, 0))],
        out_specs=[pl.BlockSpec((8, DEST_T), lambda i: (0, i)), pl.BlockSpec((8, BLK_LANES), lambda i: (0, 0))],
        out_shape=[jax.ShapeDtypeStruct((8, s), jnp.int32), jax.ShapeDtypeStruct((8, BLK_LANES), jnp.int32)],
        compiler_params=_cparams(1),
        name="dest",
    )(cnt, eid, rank, lt)


def _row_copy(src_ref, src_row, dst_ref, dst_row, sem):
    def rows(row):
        start = row * ROW_CHUNKS
        return pl.ds(start if isinstance(row, int) else pl.multiple_of(start, ROW_CHUNKS), ROW_CHUNKS)

    return pltpu.make_async_copy(src_ref.at[rows(src_row)], dst_ref.at[rows(dst_row)], sem)


def _dispatch_kernel(dest_ref, h_ref, xs_in_ref, xs_ref, sem):
    del xs_in_ref
    base = pl.program_id(0) * DISP_T

    def issue(r, carry):
        for k in range(TOP_K):
            _row_copy(h_ref, r, xs_ref, dest_ref[k * SEQ + base + r], sem).start()
        return carry

    lax.fori_loop(0, DISP_T, issue, 0)

    def drain(r, carry):
        _row_copy(h_ref, 0, xs_ref, 0, sem).wait()
        return carry

    lax.fori_loop(0, DISP_T * TOP_K, drain, 0)


def _dispatch(dest_flat, h16, xs_zero):
    return pl.pallas_call(
        _dispatch_kernel,
        grid_spec=pltpu.PrefetchScalarGridSpec(
            num_scalar_prefetch=1, grid=(SEQ // DISP_T,),
            in_specs=[pl.BlockSpec((DISP_T * ROW_CHUNKS, LANES), lambda i, d: (i, 0)),
                      pl.BlockSpec(memory_space=pl.ANY)],
            out_specs=pl.BlockSpec(memory_space=pl.ANY),
            scratch_shapes=[pltpu.SemaphoreType.DMA(())]),
        out_shape=jax.ShapeDtypeStruct(xs_zero.shape, xs_zero.dtype),
        input_output_aliases={2: 0},
        compiler_params=_cparams(1),
        name="dispatch",
    )(dest_flat, h16, xs_zero)


def _moe_kernel(be_ref, nu_ref, xs_ref, wgu_ref, wout_ref, y_ref, lhs_ref):
    del be_ref
    used = pl.program_id(0) < nu_ref[0]

    @pl.when(jnp.logical_not(used))
    def _():
        y_ref[...] = jnp.zeros_like(y_ref)

    @pl.when(used)
    def _():
        for c in range(ROW_CHUNKS):
            lhs_ref[:, c * LANES:(c + 1) * LANES] = xs_ref[pl.ds(c, MOE_R, stride=ROW_CHUNKS), :].astype(BF16)
        gu = _dot(lhs_ref[...], wgu_ref[...])
        hmid = (jax.nn.silu(gu[:, :D_FF_EXPERT]) * gu[:, D_FF_EXPERT:]).astype(BF16)
        y = _dot(hmid, wout_ref[...])
        for c in range(ROW_CHUNKS):
            y_ref[pl.ds(c, MOE_R, stride=ROW_CHUNKS), :] = y[:, c * LANES:(c + 1) * LANES]


def _moe(blk_e, n_used, xs, wgu, wout):
    last = lambda b, nu: jnp.minimum(b, nu[0] - 1)
    rows = pl.BlockSpec((MOE_R * ROW_CHUNKS, LANES), lambda b, be, nu: (last(b, nu), 0))
    return pl.pallas_call(
        _moe_kernel,
        grid_spec=pltpu.PrefetchScalarGridSpec(
            num_scalar_prefetch=2, grid=(MOE_NB,),
            in_specs=[rows,
                      pl.BlockSpec((None,) + wgu.shape[1:], lambda b, be, nu: (be[last(b, nu)], 0, 0)),
                      pl.BlockSpec((None,) + wout.shape[1:], lambda b, be, nu: (be[last(b, nu)], 0, 0))],
            out_specs=pl.BlockSpec((MOE_R * ROW_CHUNKS, LANES), lambda b, be, nu: (b, 0)),
            scratch_shapes=[pltpu.VMEM((MOE_R, D_MODEL), BF16)]),
        out_shape=jax.ShapeDtypeStruct(xs.shape, F32),
        compiler_params=_cparams(1),
        name="moe",
    )(blk_e, n_used, xs, wgu, wout)


def _combine_kernel(dest_ref, y_ref, x1_ref, wt_ref, g_ref, b_ref, o_ref, buf_ref, sem):
    base = pl.program_id(0) * COMB_T

    def issue(r, carry):
        for k in range(TOP_K):
            _row_copy(y_ref, dest_ref[k * SEQ + base + r], buf_ref.at[k], r, sem).start()
        return carry

    lax.fori_loop(0, COMB_T, issue, 0)

    def drain(r, carry):
        _row_copy(y_ref, 0, buf_ref.at[0], 0, sem).wait()
        return carry

    lax.fori_loop(0, COMB_T * TOP_K, drain, 0)

    wcol = wt_ref[...].T
    w0 = wcol[:, 0:1]
    w1 = wcol[:, 1:2]
    ffn = jnp.concatenate(
        [w0 * buf_ref[0, pl.ds(c, COMB_T, stride=ROW_CHUNKS), :] + w1 * buf_ref[1, pl.ds(c, COMB_T, stride=ROW_CHUNKS), :]
         for c in range(ROW_CHUNKS)], axis=1)
    o_ref[...] = _layer_norm(DEEPNORM_ALPHA * x1_ref[...] + ffn, g_ref[...], b_ref[...])


def _combine(dest_flat, y16, x1, wt, g, b):
    tm = COMB_T
    return pl.pallas_call(
        _combine_kernel,
        grid_spec=pltpu.PrefetchScalarGridSpec(
            num_scalar_prefetch=1, grid=(SEQ // tm,),
            in_specs=[pl.BlockSpec(memory_space=pl.ANY),
                      pl.BlockSpec((tm, D_MODEL), lambda i, d: (i, 0)),
                      pl.BlockSpec((8, tm), lambda i, d: (0, i)),
                      pl.BlockSpec(g.shape, lambda i, d: (0, 0)), pl.BlockSpec(b.shape, lambda i, d: (0, 0))],
            out_specs=pl.BlockSpec((tm, D_MODEL), lambda i, d: (i, 0)),
            scratch_shapes=[pltpu.VMEM((TOP_K, tm * ROW_CHUNKS, LANES), F32), pltpu.SemaphoreType.DMA(())]),
        out_shape=jax.ShapeDtypeStruct(x1.shape, F32),
        compiler_params=_cparams(1),
        name="combine",
    )(dest_flat, y16, x1, wt, g, b)


def _rot_matrix(width, groups, half):
    r = np.zeros((width, width), np.float32)
    for o in groups:
        for j in range(half):
            r[o + j + half, o + j] = -1.0
            r[o + j, o + half + j] = 1.0
    return jnp.asarray(r, BF16)


def _rope_tables():
    pos = jnp.arange(SEQ, dtype=F32)

    def cs(dim):
        inv_freq = ROPE_THETA ** (-jnp.arange(0, dim, 2, dtype=F32) / dim)
        ang = pos[:, None] * inv_freq[None, :]
        return jnp.cos(ang), jnp.sin(ang)

    cp, sp = cs(DIFF_ROT_DIM)
    one = jnp.ones((SEQ, DIFF_HEAD_DIM - DIFF_ROT_DIM), F32)
    cd = jnp.tile(jnp.concatenate([cp, cp, one], axis=1), (1, 2 * DIFF_HEADS))
    sd = jnp.tile(jnp.concatenate([sp, sp, 0.0 * one], axis=1), (1, 2 * DIFF_HEADS))
    cm, sm = cs(MLA_ROPE_DIM)
    scale = MLA_QK_DIM ** -0.5 * LOG2E
    ones_n = jnp.ones((SEQ, MLA_NOPE_DIM), F32)
    pad = jnp.zeros((SEQ, MLA_QK_PAD - MLA_QK_DIM), F32)
    cmq = jnp.tile(jnp.concatenate([ones_n, cm, cm, pad], axis=1) * scale, (1, MLA_HEADS))
    smq = jnp.tile(jnp.concatenate([0.0 * ones_n, sm, sm, pad], axis=1) * scale, (1, MLA_HEADS))
    ckp = jnp.concatenate([cm, cm, pad], axis=1)
    skp = jnp.concatenate([sm, sm, pad], axis=1)
    return cd, sd, cmq, smq, ckp, skp


def _constants():
    rd = _rot_matrix(512, [c * DIFF_HEAD_DIM for c in range(2 * DIFF_HEADS)], DIFF_ROT_DIM // 2)
    rmq = _rot_matrix(MLA_HEADS * MLA_QK_PAD, [h * MLA_QK_PAD + MLA_NOPE_DIM for h in range(MLA_HEADS)],
                      MLA_ROPE_DIM // 2)
    rkp = _rot_matrix(LANES, [0], MLA_ROPE_DIM // 2)
    tri = np.arange(CUM_T)
    u_incl = jnp.asarray(tri[:, None] <= tri[None, :], BF16)
    tri = np.arange(RANK_T)
    u_strict = jnp.asarray(tri[:, None] < tri[None, :], BF16)
    tri = np.arange(N_EXPERTS)
    l_incl = jnp.asarray(tri[:, None] >= tri[None, :], BF16)
    return (rd, rmq, rkp), u_incl, u_strict, l_incl


def kernel(x, w_in, b_fox_f, b_gate, diff_lambda, g_diff, g_mla_q, g_mla_kv, w_mla_uq, w_mla_ukv, w_fox_up, w_diff_up, w_mla_up, w_o, ln1_g, ln1_b, ln2_g, ln2_b, w_router, router_bias, w_exp_in, w_exp_out):
    s = x.shape[1]
    assert x.shape == (1, SEQ, D_MODEL) and w_in.shape == (DEPTH, D_MODEL, IN_WIDTH)
    xf = x.reshape(s, D_MODEL)
    tabs = _rope_tables()
    rmats, u_incl, u_strict, l_incl = _constants()
    w_a, w_s, w_g = _repack(w_in)

    perm = np.array([g * EXPERTS_PER_GROUP + j for j in range(EXPERTS_PER_GROUP) for g in range(N_GROUPS)])
    wr = w_router.T[perm]
    rb = router_bias[perm].reshape(N_EXPERTS, 1)
    zero_a = jnp.zeros((1, W_ATTN), F32)
    zero_s = jnp.zeros((1, LANES), F32)

    for i in range(DEPTH):
        pa = _linear(xf, w_a, i, zero_a, None, BF16, LIN_TN, "inproj_attn")
        ps = _linear(xf, w_s, i, zero_s, None, F32, LANES, "inproj_small")
        gates = _linear(xf, w_g, i, b_gate[i].reshape(1, -1), "sigmoid", BF16, LIN_TN, "inproj_gate")

        uq = w_mla_uq[i].reshape(MLA_Q_RANK, MLA_HEADS, MLA_QK_DIM)
        wuq = jnp.concatenate([uq, jnp.zeros((MLA_Q_RANK, MLA_HEADS, MLA_QK_PAD - MLA_QK_DIM), F32)], axis=2)
        wuq = wuq.reshape(MLA_Q_RANK, MLA_HEADS * MLA_QK_PAD).astype(BF16)
        ukv = w_mla_ukv[i].reshape(MLA_KV_RANK, MLA_HEADS, MLA_NOPE_DIM + MLA_V_DIM)
        wukv = jnp.concatenate([ukv[:, :, :MLA_NOPE_DIM].reshape(MLA_KV_RANK, -1),
                                ukv[:, :, MLA_NOPE_DIM:].reshape(MLA_KV_RANK, -1)], axis=1).astype(BF16)
        bf = jnp.zeros((1, LANES), F32).at[0, MLA_ROPE_DIM:MLA_ROPE_DIM + FOX_HEADS].set(b_fox_f[i])

        dq, dk, dvt, mq, mk, mvt, lft = _prep(pa, ps, tabs, g_mla_q[i].reshape(1, -1), g_mla_kv[i].reshape(1, -1),
                                              wuq, wukv, rmats, bf)
        cum = _cumsum(lft, u_incl)
        fq, fk, fvt = _foxprep(pa, cum)
        cum_row = cum[:FOX_HEADS].reshape(FOX_HEADS, 1, s)

        lam_init = 0.8 - 0.6 * math.exp(-0.3 * i)
        o_fox = _flash("fox", fq, fk, fvt, FOX_QK_PAD, FOX_HEADS, extra=(cum_row,))
        o_diff = _flash("diff", dq, dk, dvt, 2 * DIFF_HEAD_DIM, DIFF_HEADS,
                        extra=(diff_lambda[i], g_diff[i].reshape(1, -1)), lam_init=lam_init)
        o_mla = _flash("mla", mq, mk, mvt, MLA_QK_PAD, MLA_HEADS)

        x1, h16, eid, wt = _post(o_fox, o_diff, o_mla, gates, xf,
                                 w_fox_up[i].astype(BF16), w_diff_up[i].astype(BF16), w_mla_up[i].astype(BF16),
                                 w_o[i].astype(BF16), ln1_g[i].reshape(1, -1), ln1_b[i].reshape(1, -1), wr, rb)

        rank, cnt = _rank(eid, u_strict)
        dest, blk = _dest(cnt, eid, rank, l_incl)
        dest_flat = dest[:TOP_K].reshape(TOP_K * s)
        xs = _dispatch(dest_flat, h16, jnp.zeros((MOE_ROWS * ROW_CHUNKS, LANES), F32))
        y16 = _moe(blk[0], blk[1, :1], xs, w_exp_in[i].astype(BF16), w_exp_out[i].astype(BF16))
        xf = _combine(dest_flat, y16, x1, wt, ln2_g[i].reshape(1, -1), ln2_b[i].reshape(1, -1))
    return xf.reshape(1, s, D_MODEL)
```

```python
import functools
import math

import numpy as np
import jax
import jax.numpy as jnp
from jax import lax
from jax.experimental import pallas as pl
from jax.experimental.pallas import tpu as pltpu

F32 = jnp.float32
BF16 = jnp.bfloat16

D_MODEL = 2048
SEQ = 16384
DEPTH = 2
ROPE_THETA = 500000.0
NORM_EPS = 1e-5
FOX_HEADS = 4
FOX_HEAD_DIM = 128
DIFF_HEADS = 4
DIFF_HEAD_DIM = 64
DIFF_V_DIM = 128
DIFF_ROT_DIM = 16
MLA_HEADS = 4
MLA_Q_RANK = 512
MLA_KV_RANK = 512
MLA_NOPE_DIM = 128
MLA_ROPE_DIM = 64
MLA_V_DIM = 128
MLA_QK_DIM = MLA_NOPE_DIM + MLA_ROPE_DIM
N_EXPERTS = 32
N_GROUPS = 8
EXPERTS_PER_GROUP = 4
TOP_K = 2
D_FF_EXPERT = 1408
DEEPNORM_ALPHA = (2 * DEPTH) ** 0.25

LANES = 128
ROW_CHUNKS = D_MODEL // LANES
MLA_QK_PAD = 256
FOX_QK_PAD = 256
BF16_ROWS = 16
VT_ROWS = LANES + BF16_ROWS
LOG2E = math.log2(math.e)
VMEM_LIMIT = 56 * 1024 * 1024

LIN_TM = 1024
LIN_TN = 1024
PREP_TM = 512
CUM_T = 512
ATT_T = 512
ATT_TQ = 2 * ATT_T
POST_TM = 256
RANK_T = 512
DEST_T = 2048
DISP_T = 512
MOE_R = 256
MOE_NB = (SEQ * TOP_K + N_EXPERTS * (MOE_R - 1) + MOE_R - 1) // MOE_R
MOE_ROWS = MOE_NB * MOE_R
BLK_LANES = 256
COMB_T = 256
NEG = -1e30


def _cparams(n_axes, **kw):
    return pltpu.CompilerParams(dimension_semantics=("arbitrary",) * n_axes,
                                vmem_limit_bytes=VMEM_LIMIT, **kw)


def _dot(a, b):
    return jnp.dot(a, b, preferred_element_type=F32)


def _dot_nt(a, b):
    return lax.dot_general(a, b, (((1,), (1,)), ((), ())), preferred_element_type=F32)


def _split3(x):
    a = x.astype(BF16)
    r = x - a.astype(F32)
    b = r.astype(BF16)
    c = (r - b.astype(F32)).astype(BF16)
    return a, b, c


def _linear_kernel(x_ref, w_ref, b_ref, o_ref, xb_ref, *, act):
    @pl.when(pl.program_id(1) == 0)
    def _():
        xb_ref[...] = x_ref[...].astype(BF16)

    acc = _dot(xb_ref[...], w_ref[...]) + b_ref[...]
    if act == "sigmoid":
        acc = jax.nn.sigmoid(acc)
    o_ref[...] = acc.astype(o_ref.dtype)


def _linear(x, w, layer, b, act, out_dtype, tn, name):
    m, k = x.shape
    n = w.shape[2]
    return pl.pallas_call(
        functools.partial(_linear_kernel, act=act),
        grid=(m // LIN_TM, n // tn),
        in_specs=[pl.BlockSpec((LIN_TM, k), lambda i, j: (i, 0)),
                  pl.BlockSpec((None, k, tn), lambda i, j: (layer, 0, j)),
                  pl.BlockSpec((1, tn), lambda i, j: (0, j))],
        out_specs=pl.BlockSpec((LIN_TM, tn), lambda i, j: (i, j)),
        out_shape=jax.ShapeDtypeStruct((m, n), out_dtype),
        scratch_shapes=[pltpu.VMEM((LIN_TM, k), BF16)],
        compiler_params=_cparams(2),
        name=name,
    )(x, w, b)


O_FOX_F = 3 * 512
O_DIFF = O_FOX_F + FOX_HEADS
O_MLA = O_DIFF + 3 * 512
O_KPE = O_MLA + 2 * 512
O_GATE = O_KPE + MLA_ROPE_DIM
IN_WIDTH = O_GATE + 3 * D_MODEL
W_ATTN = 4096
REPACK_TK = 256


def _repack_kernel(w_ref, wa_ref, ws_ref, wg_ref):
    wa_ref[:, 0:O_FOX_F] = w_ref[:, 0:O_FOX_F].astype(BF16)
    wa_ref[:, O_FOX_F:W_ATTN] = w_ref[:, O_DIFF:O_KPE].astype(BF16)
    ws_ref[...] = jnp.zeros_like(ws_ref)
    ws_ref[:, 0:MLA_ROPE_DIM] = w_ref[:, O_KPE:O_GATE].astype(BF16)
    ws_ref[:, MLA_ROPE_DIM:MLA_ROPE_DIM + FOX_HEADS] = w_ref[:, O_FOX_F:O_DIFF].astype(BF16)
    wg_ref[...] = w_ref[:, O_GATE:IN_WIDTH].astype(BF16)


def _repack(w_in):
    k = w_in.shape[1]
    blk = lambda n: pl.BlockSpec((None, REPACK_TK, n), lambda l, i: (l, i, 0))
    return pl.pallas_call(
        _repack_kernel,
        grid=(DEPTH, k // REPACK_TK),
        in_specs=[blk(IN_WIDTH)],
        out_specs=[blk(W_ATTN), blk(LANES), blk(3 * D_MODEL)],
        out_shape=[jax.ShapeDtypeStruct((DEPTH, k, W_ATTN), BF16), jax.ShapeDtypeStruct((DEPTH, k, LANES), BF16),
                   jax.ShapeDtypeStruct((DEPTH, k, 3 * D_MODEL), BF16)],
        compiler_params=_cparams(2),
        name="repack",
    )(w_in)


def _rms(xf, g):
    ms = jnp.mean(jnp.square(xf), axis=-1, keepdims=True)
    return xf * lax.rsqrt(ms + NORM_EPS) * g


def _rope(x_f32, x_b16, r_ref, c, s):
    return x_f32 * c + _dot(x_b16, r_ref[...]) * s


def _store_vt(vt_ref, v, heads):
    vt = v.T.astype(BF16)
    ones = jnp.ones((BF16_ROWS, v.shape[0]), BF16)
    for h in range(heads):
        vt_ref[h * VT_ROWS:h * VT_ROWS + LANES, :] = vt[h * LANES:(h + 1) * LANES]
        vt_ref[h * VT_ROWS + LANES:(h + 1) * VT_ROWS, :] = ones


def _prep_kernel(dq_ref, dk_ref, dv_ref, cq_ref, ckv_ref, ps_ref, cd_ref, sd_ref, cmq_ref, smq_ref, ckp_ref, skp_ref,
                 gq_ref, gkv_ref, wuq_ref, wukv_ref, rd_ref, rmq_ref, rkp_ref, bf_ref,
                 dqo_ref, dko_ref, dvt_ref, mq_ref, mk_ref, mvt_ref, lft_ref):
    cd = cd_ref[...]
    sd = sd_ref[...]
    dq = dq_ref[...]
    dk = dk_ref[...]
    dqo_ref[...] = (_rope(dq.astype(F32), dq, rd_ref, cd, sd) * (DIFF_HEAD_DIM ** -0.5 * LOG2E)).astype(BF16)
    dko_ref[...] = _rope(dk.astype(F32), dk, rd_ref, cd, sd).astype(BF16)
    _store_vt(dvt_ref, dv_ref[...].astype(F32), DIFF_HEADS)

    cqn = _rms(cq_ref[...].astype(F32), gq_ref[...]).astype(BF16)
    q = _dot(cqn, wuq_ref[...])
    mq_ref[...] = _rope(q, q.astype(BF16), rmq_ref, cmq_ref[...], smq_ref[...]).astype(BF16)

    ckvn = _rms(ckv_ref[...].astype(F32), gkv_ref[...]).astype(BF16)
    kv = _dot(ckvn, wukv_ref[...])
    ps = ps_ref[...]
    kpe = _rope(ps, ps.astype(BF16), rkp_ref, ckp_ref[...], skp_ref[...]).astype(BF16)
    for h in range(MLA_HEADS):
        mk_ref[:, h * MLA_QK_PAD:h * MLA_QK_PAD + LANES] = kv[:, h * LANES:(h + 1) * LANES].astype(BF16)
        mk_ref[:, h * MLA_QK_PAD + LANES:(h + 1) * MLA_QK_PAD] = kpe
    _store_vt(mvt_ref, kv[:, MLA_HEADS * LANES:], MLA_HEADS)

    z = ps + bf_ref[...]
    lf = jnp.minimum(z, 0.0) - jnp.log1p(jnp.exp(-jnp.abs(z)))
    lft_ref[...] = lf.T[MLA_ROPE_DIM:MLA_ROPE_DIM + 8, :]


def _prep(pa, ps, tabs, gq, gkv, wuq, wukv, rmats, bf):
    s = pa.shape[0]
    tm = PREP_TM
    row = lambda w: pl.BlockSpec((tm, w), lambda i: (i, 0))
    col = lambda w, c: pl.BlockSpec((tm, w), lambda i: (i, c))
    tr = lambda r: pl.BlockSpec((r, tm), lambda i: (0, i))
    full = lambda a: pl.BlockSpec(a.shape, lambda i: (0,) * a.ndim)
    cd, sd, cmq, smq, ckp, skp = tabs
    rd, rmq, rkp = rmats
    return pl.pallas_call(
        _prep_kernel,
        grid=(s // tm,),
        in_specs=[col(512, 3), col(512, 4), col(512, 5), col(512, 6), col(512, 7), row(LANES),
                  row(512), row(512), row(1024), row(1024), row(LANES), row(LANES),
                  full(gq), full(gkv), full(wuq), full(wukv), full(rd), full(rmq), full(rkp), full(bf)],
        out_specs=[row(512), row(512), tr(4 * VT_ROWS), row(1024), row(1024), tr(4 * VT_ROWS), tr(8)],
        out_shape=[jax.ShapeDtypeStruct((s, 512), BF16), jax.ShapeDtypeStruct((s, 512), BF16),
                   jax.ShapeDtypeStruct((4 * VT_ROWS, s), BF16),
                   jax.ShapeDtypeStruct((s, 1024), BF16), jax.ShapeDtypeStruct((s, 1024), BF16),
                   jax.ShapeDtypeStruct((4 * VT_ROWS, s), BF16), jax.ShapeDtypeStruct((8, s), F32)],
        compiler_params=_cparams(1),
        name="prep",
    )(pa, pa, pa, pa, pa, ps, cd, sd, cmq, smq, ckp, skp, gq, gkv, wuq, wukv, rd, rmq, rkp, bf)


def _cumsum_kernel(x_ref, u_ref, o_ref, carry_ref):
    @pl.when(pl.program_id(0) == 0)
    def _():
        carry_ref[...] = jnp.zeros_like(carry_ref)

    a, b, c = _split3(x_ref[...])
    u = u_ref[...]
    cs = _dot(a, u) + _dot(b, u) + _dot(c, u) + carry_ref[:, :1]
    o_ref[...] = cs
    carry_ref[...] = jnp.broadcast_to(cs[:, CUM_T - 1:CUM_T], carry_ref.shape)


def _cumsum(x, u):
    s = x.shape[1]
    return pl.pallas_call(
        _cumsum_kernel,
        grid=(s // CUM_T,),
        in_specs=[pl.BlockSpec((8, CUM_T), lambda i: (0, i)), pl.BlockSpec(u.shape, lambda i: (0, 0))],
        out_specs=pl.BlockSpec((8, CUM_T), lambda i: (0, i)),
        out_shape=jax.ShapeDtypeStruct((8, s), F32),
        scratch_shapes=[pltpu.VMEM((8, LANES), F32)],
        compiler_params=_cparams(1),
        name="cumsum",
    )(x, u)


N_BIAS = 3


def _foxprep_kernel(q_ref, k_ref, v_ref, cum_ref, qo_ref, ko_ref, vt_ref):
    cum = cum_ref[...]
    beta = (cum[:, 0:1] - cum) * LOG2E
    pieces = [p.astype(F32).T for p in _split3(beta)]
    lane = lax.broadcasted_iota(jnp.int32, (ATT_T, LANES), 1)
    ones = jnp.where(lane < N_BIAS, 1.0, 0.0).astype(BF16)
    q = q_ref[...].astype(F32) * (FOX_HEAD_DIM ** -0.5 * LOG2E)
    for h in range(FOX_HEADS):
        e = jnp.zeros((ATT_T, LANES), F32)
        for n, p in enumerate(pieces):
            e = jnp.where(lane == n, p[:, h:h + 1], e)
        ko_ref[:, h * FOX_QK_PAD:h * FOX_QK_PAD + LANES] = k_ref[:, h * LANES:(h + 1) * LANES]
        ko_ref[:, h * FOX_QK_PAD + LANES:(h + 1) * FOX_QK_PAD] = e.astype(BF16)
        qo_ref[:, h * FOX_QK_PAD:h * FOX_QK_PAD + LANES] = q[:, h * LANES:(h + 1) * LANES].astype(BF16)
        qo_ref[:, h * FOX_QK_PAD + LANES:(h + 1) * FOX_QK_PAD] = ones
    _store_vt(vt_ref, v_ref[...].astype(F32), FOX_HEADS)


def _foxprep(pa, cum):
    s = pa.shape[0]
    t = ATT_T
    w = FOX_HEADS * FOX_QK_PAD
    return pl.pallas_call(
        _foxprep_kernel,
        grid=(s // t,),
        in_specs=[pl.BlockSpec((t, 512), lambda i: (i, 0)), pl.BlockSpec((t, 512), lambda i: (i, 1)),
                  pl.BlockSpec((t, 512), lambda i: (i, 2)), pl.BlockSpec((8, t), lambda i: (0, i))],
        out_specs=[pl.BlockSpec((t, w), lambda i: (i, 0)), pl.BlockSpec((t, w), lambda i: (i, 0)),
                   pl.BlockSpec((FOX_HEADS * VT_ROWS, t), lambda i: (0, i))],
        out_shape=[jax.ShapeDtypeStruct((s, w), BF16), jax.ShapeDtypeStruct((s, w), BF16),
                   jax.ShapeDtypeStruct((FOX_HEADS * VT_ROWS, s), BF16)],
        compiler_params=_cparams(1),
        name="foxprep",
    )(pa, pa, pa, cum)


def _flash_kernel(*refs, mode, lam_init):
    t = ATT_T
    if mode == "fox":
        q_ref, k_ref, vt_ref, ck_ref, o_ref, sa_ref, sb_ref, m_ref, acc_ref = refs
    elif mode == "diff":
        q_ref, k_ref, vt_ref, lam_ref, gd_ref, o_ref, sa_ref, sb_ref, m_ref, acc_ref = refs
    else:
        q_ref, k_ref, vt_ref, o_ref, sa_ref, sb_ref, m_ref, acc_ref = refs
    i = pl.program_id(1)
    off_i = pl.multiple_of(i * ATT_TQ, ATT_TQ)
    m_ref[...] = jnp.full(m_ref.shape, NEG, F32)
    acc_ref[...] = jnp.zeros_like(acc_ref)

    q = q_ref[...]
    if mode == "diff":
        lane = lax.broadcasted_iota(jnp.int32, q.shape, 1)
        zero = jnp.zeros_like(q)
        qs = [jnp.where(lane < DIFF_HEAD_DIM, q, zero), jnp.where(lane >= DIFF_HEAD_DIM, q, zero)]
    else:
        qs = [q]

    def scores(off, s_ref):
        kb = k_ref[pl.ds(off, t), :]
        for n, qn in enumerate(qs):
            s_ref[n] = _dot_nt(kb, qn)

    def consume(off, s_ref, masked):
        vtb = vt_ref[:, pl.ds(off, t)]
        for n in range(len(qs)):
            st = s_ref[n]
            if masked:
                key_id = lax.broadcasted_iota(jnp.int32, st.shape, 0) + (off - off_i)
                qry_id = lax.broadcasted_iota(jnp.int32, st.shape, 1)
                st = jnp.where(key_id <= qry_id, st, NEG)
            m = jnp.max(st, axis=0, keepdims=True)
            p = jnp.exp2((st - m).astype(BF16))
            o = _dot(vtb, p)
            if mode == "fox":
                m = m + (ck_ref[:, pl.ds(off_i, LANES)][:, :1] - ck_ref[:, pl.ds(off, LANES)][:, :1]) * LOG2E
            m_prev = m_ref[n]
            m_new = jnp.maximum(m_prev, m)
            acc_ref[n] = jnp.exp2(m_prev - m_new) * acc_ref[n] + jnp.exp2(m - m_new) * o
            m_ref[n] = m_new

    scores(0, sa_ref)

    def pair(jj, carry):
        off0 = pl.multiple_of(jj * (2 * t), 2 * t)
        off1 = pl.multiple_of(off0 + t, t)
        scores(off1, sb_ref)
        consume(off0, sa_ref, False)
        scores(pl.multiple_of(off1 + t, t), sa_ref)
        consume(off1, sb_ref, False)
        return carry

    lax.fori_loop(0, i, pair, 0)
    off_d = pl.multiple_of(off_i + t, t)
    scores(off_d, sb_ref)
    consume(off_i, sa_ref, True)
    consume(off_d, sb_ref, True)

    def normalised(n):
        acc = acc_ref[n]
        return acc[:LANES] / acc[LANES:LANES + 1]

    if mode == "diff":
        dl = lam_ref[...]
        lam = (jnp.exp(jnp.sum(dl[0:1] * dl[1:2], axis=1, keepdims=True))
               - jnp.exp(jnp.sum(dl[2:3] * dl[3:4], axis=1, keepdims=True)) + lam_init)
        o = _rms((normalised(0) - lam * normalised(1)).T, gd_ref[...]) * (1.0 - lam_init)
    else:
        o = normalised(0).T
    o_ref[...] = o.astype(o_ref.dtype)


def _flash(mode, q_arr, k_arr, vt_arr, dk, heads, extra=(), lam_init=0.0):
    s = q_arr.shape[0]
    t = ATT_T
    tq = ATT_TQ
    dv = LANES
    n_stream = 2 if mode == "diff" else 1
    in_specs = [pl.BlockSpec((tq, dk), lambda h, i: (i, h)),
                pl.BlockSpec((s, dk), lambda h, i: (0, h)),
                pl.BlockSpec((VT_ROWS, s), lambda h, i: (h, 0))]
    if mode == "fox":
        in_specs.append(pl.BlockSpec((None, 1, s), lambda h, i: (h, 0, 0)))
    elif mode == "diff":
        in_specs += [pl.BlockSpec(extra[0].shape, lambda h, i: (0, 0)),
                     pl.BlockSpec(extra[1].shape, lambda h, i: (0, 0))]
    return pl.pallas_call(
        functools.partial(_flash_kernel, mode=mode, lam_init=lam_init),
        grid=(heads, s // tq),
        in_specs=in_specs,
        out_specs=pl.BlockSpec((tq, dv), lambda h, i: (i, h)),
        out_shape=jax.ShapeDtypeStruct((s, heads * dv), BF16),
        scratch_shapes=[pltpu.VMEM((n_stream, t, tq), F32), pltpu.VMEM((n_stream, t, tq), F32),
                        pltpu.VMEM((n_stream, 1, tq), F32), pltpu.VMEM((n_stream, VT_ROWS, tq), F32)],
        compiler_params=_cparams(2),
        name="flash_" + mode,
    )(q_arr, k_arr, vt_arr, *extra)


def _layer_norm(z, g, b):
    mu = jnp.mean(z, axis=-1, keepdims=True)
    zc = z - mu
    var = jnp.mean(jnp.square(zc), axis=-1, keepdims=True)
    return zc * lax.rsqrt(var + NORM_EPS) * g + b


def _first_index_of_max(vals):
    best = vals[0]
    for v in vals[1:]:
        best = jnp.maximum(best, v)
    idx = jnp.full(best.shape, len(vals) - 1, jnp.int32)
    for j in range(len(vals) - 2, -1, -1):
        idx = jnp.where(vals[j] == best, j, idx)
    return best, idx


def _post_kernel(of_ref, od_ref, om_ref, g0_ref, g1_ref, g2_ref, x_ref, wf_ref, wd_ref, wm_ref, wo_ref,
                 lg_ref, lb_ref, wr_ref, rb_ref, x1_ref, h16_ref, eid_ref, wt_ref):
    tm = POST_TM
    merged = (g0_ref[...].astype(F32) * _dot(of_ref[...], wf_ref[...])
              + g1_ref[...].astype(F32) * _dot(od_ref[...], wd_ref[...])
              + g2_ref[...].astype(F32) * _dot(om_ref[...], wm_ref[...]))
    mix = _dot(merged.astype(BF16), wo_ref[...])
    x1 = _layer_norm(DEEPNORM_ALPHA * x_ref[...] + mix, lg_ref[...], lb_ref[...])
    x1_ref[...] = x1
    for c in range(ROW_CHUNKS):
        h16_ref[pl.ds(c, tm, stride=ROW_CHUNKS), :] = x1[:, c * LANES:(c + 1) * LANES]

    e = N_EXPERTS
    prod = [_dot(p, wr_ref[...]).T for p in _split3(x1)]
    logits = (prod[0][0:e] + (prod[0][e:2 * e] + prod[1][0:e])
              + (prod[0][2 * e:3 * e] + prod[1][e:2 * e] + prod[2][0:e]))
    scores = jax.nn.sigmoid(logits)
    biased = scores + rb_ref[...]
    a = [biased[j * 8:(j + 1) * 8] for j in range(EXPERTS_PER_GROUP)]
    u = [scores[j * 8:(j + 1) * 8] for j in range(EXPERTS_PER_GROUP)]
    gs = a[0] + a[1]
    for j0 in range(EXPERTS_PER_GROUP):
        for j1 in range(j0 + 1, EXPERTS_PER_GROUP):
            gs = jnp.maximum(gs, a[j0] + a[j1])
    gmax = jnp.max(gs, axis=0, keepdims=True)
    gid = lax.broadcasted_iota(jnp.int32, gs.shape, 0)
    best = jnp.min(jnp.where(gs == gmax, gid, N_GROUPS), axis=0, keepdims=True)
    sel = gid == best
    ing = [jnp.sum(jnp.where(sel, a[j], 0.0), axis=0, keepdims=True) for j in range(EXPERTS_PER_GROUP)]
    unb = [jnp.sum(jnp.where(sel, u[j], 0.0), axis=0, keepdims=True) for j in range(EXPERTS_PER_GROUP)]
    _, l1 = _first_index_of_max(ing)
    ing2 = [jnp.where(l1 == j, -jnp.inf, ing[j]) for j in range(EXPERTS_PER_GROUP)]
    _, l2 = _first_index_of_max(ing2)
    w1 = jnp.zeros_like(unb[0])
    w2 = jnp.zeros_like(unb[0])
    for j in range(EXPERTS_PER_GROUP):
        w1 = jnp.where(l1 == j, unb[j], w1)
        w2 = jnp.where(l2 == j, unb[j], w2)
    tot = w1 + w2
    eid_ref[...] = jnp.zeros_like(eid_ref)
    wt_ref[...] = jnp.zeros_like(wt_ref)
    eid_ref[0:1, :] = best * EXPERTS_PER_GROUP + l1
    eid_ref[1:2, :] = best * EXPERTS_PER_GROUP + l2
    wt_ref[0:1, :] = w1 / tot
    wt_ref[1:2, :] = w2 / tot


def _post(o_fox, o_diff, o_mla, gates, x, wf, wd, wm, wo, lg, lb, wr, rb):
    s = x.shape[0]
    tm = POST_TM
    row = lambda w: pl.BlockSpec((tm, w), lambda i: (i, 0))
    full = lambda a: pl.BlockSpec(a.shape, lambda i: (0,) * a.ndim, pipeline_mode=pl.Buffered(1))
    return pl.pallas_call(
        _post_kernel,
        grid=(s // tm,),
        in_specs=[row(512), row(512), row(512),
                  pl.BlockSpec((tm, D_MODEL), lambda i: (i, 0)),
                  pl.BlockSpec((tm, D_MODEL), lambda i: (i, 1)),
                  pl.BlockSpec((tm, D_MODEL), lambda i: (i, 2)),
                  row(D_MODEL), full(wf), full(wd), full(wm), full(wo), full(lg), full(lb), full(wr), full(rb)],
        out_specs=[row(D_MODEL), pl.BlockSpec((tm * ROW_CHUNKS, LANES), lambda i: (i, 0)),
                   pl.BlockSpec((8, tm), lambda i: (0, i)), pl.BlockSpec((8, tm), lambda i: (0, i))],
        out_shape=[jax.ShapeDtypeStruct((s, D_MODEL), F32),
                   jax.ShapeDtypeStruct((s * ROW_CHUNKS, LANES), F32),
                   jax.ShapeDtypeStruct((8, s), jnp.int32), jax.ShapeDtypeStruct((8, s), F32)],
        compiler_params=_cparams(1),
        name="post",
    )(o_fox, o_diff, o_mla, gates, gates, gates, x, wf, wd, wm, wo, lg, lb, wr, rb)


def _rank_kernel(eid_ref, us_ref, rank_ref, cnt_ref, carry_ref):
    @pl.when(pl.program_id(0) == 0)
    def _():
        carry_ref[...] = jnp.zeros_like(carry_ref)

    e0 = eid_ref[0:1, :]
    e1 = eid_ref[1:2, :]
    eio = lax.broadcasted_iota(jnp.int32, (N_EXPERTS, RANK_T), 0)
    hit0 = eio == e0
    hit1 = eio == e1
    onehot = jnp.where(hit0 | hit1, 1.0, 0.0)
    before = _dot(onehot.astype(BF16), us_ref[...]) + carry_ref[:, :1]
    rank_ref[...] = jnp.zeros_like(rank_ref)
    rank_ref[0:1, :] = jnp.sum(jnp.where(hit0, before, 0.0), axis=0, keepdims=True).astype(jnp.int32)
    rank_ref[1:2, :] = jnp.sum(jnp.where(hit1, before, 0.0), axis=0, keepdims=True).astype(jnp.int32)
    carry_ref[...] = carry_ref[...] + jnp.sum(onehot, axis=1, keepdims=True)
    cnt_ref[...] = carry_ref[...]


def _rank(eid, us):
    s = eid.shape[1]
    return pl.pallas_call(
        _rank_kernel,
        grid=(s // RANK_T,),
        in_specs=[pl.BlockSpec((8, RANK_T), lambda i: (0, i)), pl.BlockSpec(us.shape, lambda i: (0, 0))],
        out_specs=[pl.BlockSpec((8, RANK_T), lambda i: (0, i)),
                   pl.BlockSpec((N_EXPERTS, LANES), lambda i: (0, 0))],
        out_shape=[jax.ShapeDtypeStruct((8, s), jnp.int32), jax.ShapeDtypeStruct((N_EXPERTS, LANES), F32)],
        scratch_shapes=[pltpu.VMEM((N_EXPERTS, LANES), F32)],
        compiler_params=_cparams(1),
        name="rank",
    )(eid, us)


def _dest_kernel(cnt_ref, eid_ref, rank_ref, lt_ref, dest_ref, blk_ref):
    nb = jnp.floor((cnt_ref[...] + (MOE_R - 1)) * (1.0 / MOE_R))
    incl = _dot(lt_ref[...], nb.astype(BF16))
    start = (incl - nb)[:, :1] * MOE_R
    eio = lax.broadcasted_iota(jnp.int32, (N_EXPERTS, DEST_T), 0)
    dest_ref[...] = jnp.zeros_like(dest_ref)
    for k in range(TOP_K):
        base = jnp.sum(jnp.where(eio == eid_ref[k:k + 1, :], start, 0.0), axis=0, keepdims=True)
        dest_ref[k:k + 1, :] = base.astype(jnp.int32) + rank_ref[k:k + 1, :]
    bid = lax.broadcasted_iota(jnp.int32, (N_EXPERTS, BLK_LANES), 1).astype(F32)
    ended = jnp.sum(jnp.where(incl[:, :1] <= bid, 1.0, 0.0), axis=0, keepdims=True)
    blk_ref[...] = jnp.zeros_like(blk_ref)
    blk_ref[0:1, :] = jnp.minimum(ended, N_EXPERTS - 1.0).astype(jnp.int32)
    blk_ref[1:2, :] = jnp.broadcast_to(incl[N_EXPERTS - 1:N_EXPERTS, :1], (1, BLK_LANES)).astype(jnp.int32)


def _dest(cnt, eid, rank, lt):
    s = eid.shape[1]
    return pl.pallas_call(
        _dest_kernel,
        grid=(s // DEST_T,),
        in_specs=[pl.BlockSpec(cnt.shape, lambda i: (0, 0)),
                  pl.BlockSpec((8, DEST_T), lambda i: (0, i)), pl.BlockSpec((8, DEST_T), lambda i: (0, i)),
                  pl.BlockSpec(lt.shape, lambda i: (0, 0))],
        out_specs=[pl.BlockSpec((8, DEST_T), lambda i: (0, i)), pl.BlockSpec((8, BLK_LANES), lambda i: (0, 0))],
        out_shape=[jax.ShapeDtypeStruct((8, s), jnp.int32), jax.ShapeDtypeStruct((8, BLK_LANES), jnp.int32)],
        compiler_params=_cparams(1),
        name="dest",
    )(cnt, eid, rank, lt)


def _row_copy(src_ref, src_row, dst_ref, dst_row, sem):
    def rows(row):
        start = row * ROW_CHUNKS
        return pl.ds(start if isinstance(row, int) else pl.multiple_of(start, ROW_CHUNKS), ROW_CHUNKS)

    return pltpu.make_async_copy(src_ref.at[rows(src_row)], dst_ref.at[rows(dst_row)], sem)


def _dispatch_kernel(dest_ref, h_ref, xs_in_ref, xs_ref, sem):
    del xs_in_ref
    base = pl.program_id(0) * DISP_T

    def issue(r, carry):
        for k in range(TOP_K):
            _row_copy(h_ref, r, xs_ref, dest_ref[k * SEQ + base + r], sem).start(priority=k)
        return carry

    lax.fori_loop(0, DISP_T, issue, 0)

    def drain(r, carry):
        _row_copy(h_ref, 0, xs_ref, 0, sem).wait()
        return carry

    lax.fori_loop(0, DISP_T * TOP_K, drain, 0)


def _dispatch(dest_flat, h16, xs_zero):
    return pl.pallas_call(
        _dispatch_kernel,
        grid_spec=pltpu.PrefetchScalarGridSpec(
            num_scalar_prefetch=1, grid=(SEQ // DISP_T,),
            in_specs=[pl.BlockSpec((DISP_T * ROW_CHUNKS, LANES), lambda i, d: (i, 0)),
                      pl.BlockSpec(memory_space=pl.ANY)],
            out_specs=pl.BlockSpec(memory_space=pl.ANY),
            scratch_shapes=[pltpu.SemaphoreType.DMA(())]),
        out_shape=jax.ShapeDtypeStruct(xs_zero.shape, xs_zero.dtype),
        input_output_aliases={2: 0},
        compiler_params=_cparams(1),
        name="dispatch",
    )(dest_flat, h16, xs_zero)


def _moe_kernel(be_ref, nu_ref, xs_ref, wgu_ref, wout_ref, y_ref, lhs_ref):
    del be_ref
    used = pl.program_id(0) < nu_ref[0]

    @pl.when(jnp.logical_not(used))
    def _():
        y_ref[...] = jnp.zeros_like(y_ref)

    @pl.when(used)
    def _():
        for c in range(ROW_CHUNKS):
            lhs_ref[:, c * LANES:(c + 1) * LANES] = xs_ref[pl.ds(c, MOE_R, stride=ROW_CHUNKS), :].astype(BF16)
        gu = _dot(lhs_ref[...], wgu_ref[...])
        hmid = (jax.nn.silu(gu[:, :D_FF_EXPERT]) * gu[:, D_FF_EXPERT:]).astype(BF16)
        y = _dot(hmid, wout_ref[...])
        for c in range(ROW_CHUNKS):
            y_ref[pl.ds(c, MOE_R, stride=ROW_CHUNKS), :] = y[:, c * LANES:(c + 1) * LANES]


def _moe(blk_e, n_used, xs, wgu, wout, layer):
    last = lambda b, nu: jnp.minimum(b, nu[0] - 1)
    rows = pl.BlockSpec((MOE_R * ROW_CHUNKS, LANES), lambda b, be, nu: (last(b, nu), 0))
    return pl.pallas_call(
        _moe_kernel,
        grid_spec=pltpu.PrefetchScalarGridSpec(
            num_scalar_prefetch=2, grid=(MOE_NB,),
            in_specs=[rows,
                      pl.BlockSpec((None, None) + wgu.shape[2:], lambda b, be, nu: (layer, be[last(b, nu)], 0, 0)),
                      pl.BlockSpec((None, None) + wout.shape[2:], lambda b, be, nu: (layer, be[last(b, nu)], 0, 0))],
            out_specs=pl.BlockSpec((MOE_R * ROW_CHUNKS, LANES), lambda b, be, nu: (b, 0)),
            scratch_shapes=[pltpu.VMEM((MOE_R, D_MODEL), BF16)]),
        out_shape=jax.ShapeDtypeStruct(xs.shape, F32),
        compiler_params=_cparams(1),
        name="moe",
    )(blk_e, n_used, xs, wgu, wout)


def _combine_kernel(dest_ref, y_ref, x1_ref, wt_ref, g_ref, b_ref, o_ref, buf_ref, sem):
    i = pl.program_id(0)
    slot = i & 1

    def issue(tile, sl):
        def body(r, carry):
            for k in range(TOP_K):
                _row_copy(y_ref, dest_ref[k * SEQ + tile * COMB_T + r], buf_ref.at[sl, k], r,
                          sem.at[sl]).start(priority=k)
            return carry

        lax.fori_loop(0, COMB_T, body, 0)

    @pl.when(i == 0)
    def _():
        issue(0, 0)

    @pl.when(i + 1 < pl.num_programs(0))
    def _():
        issue(i + 1, 1 - slot)

    def drain(r, carry):
        _row_copy(y_ref, 0, buf_ref.at[slot, 0], 0, sem.at[slot]).wait()
        return carry

    lax.fori_loop(0, COMB_T * TOP_K, drain, 0)

    wcol = wt_ref[...].T
    w0 = wcol[:, 0:1]
    w1 = wcol[:, 1:2]
    ffn = jnp.concatenate(
        [w0 * buf_ref[slot, 0, pl.ds(c, COMB_T, stride=ROW_CHUNKS), :]
         + w1 * buf_ref[slot, 1, pl.ds(c, COMB_T, stride=ROW_CHUNKS), :] for c in range(ROW_CHUNKS)], axis=1)
    o_ref[...] = _layer_norm(DEEPNORM_ALPHA * x1_ref[...] + ffn, g_ref[...], b_ref[...])


def _combine(dest_flat, y16, x1, wt, g, b):
    tm = COMB_T
    return pl.pallas_call(
        _combine_kernel,
        grid_spec=pltpu.PrefetchScalarGridSpec(
            num_scalar_prefetch=1, grid=(SEQ // tm,),
            in_specs=[pl.BlockSpec(memory_space=pl.ANY),
                      pl.BlockSpec((tm, D_MODEL), lambda i, d: (i, 0)),
                      pl.BlockSpec((8, tm), lambda i, d: (0, i)),
                      pl.BlockSpec(g.shape, lambda i, d: (0, 0)), pl.BlockSpec(b.shape, lambda i, d: (0, 0))],
            out_specs=pl.BlockSpec((tm, D_MODEL), lambda i, d: (i, 0)),
            scratch_shapes=[pltpu.VMEM((2, TOP_K, tm * ROW_CHUNKS, LANES), F32), pltpu.SemaphoreType.DMA((2,))]),
        out_shape=jax.ShapeDtypeStruct(x1.shape, F32),
        compiler_params=_cparams(1),
        name="combine",
    )(dest_flat, y16, x1, wt, g, b)


def _rot_matrix(width, groups, half):
    r = np.zeros((width, width), np.float32)
    for o in groups:
        for j in range(half):
            r[o + j + half, o + j] = -1.0
            r[o + j, o + half + j] = 1.0
    return jnp.asarray(r, BF16)


def _rope_tables():
    pos = jnp.arange(SEQ, dtype=F32)

    def cs(dim):
        inv_freq = ROPE_THETA ** (-jnp.arange(0, dim, 2, dtype=F32) / dim)
        ang = pos[:, None] * inv_freq[None, :]
        return jnp.cos(ang), jnp.sin(ang)

    cp, sp = cs(DIFF_ROT_DIM)
    one = jnp.ones((SEQ, DIFF_HEAD_DIM - DIFF_ROT_DIM), F32)
    cd = jnp.tile(jnp.concatenate([cp, cp, one], axis=1), (1, 2 * DIFF_HEADS))
    sd = jnp.tile(jnp.concatenate([sp, sp, 0.0 * one], axis=1), (1, 2 * DIFF_HEADS))
    cm, sm = cs(MLA_ROPE_DIM)
    scale = MLA_QK_DIM ** -0.5 * LOG2E
    ones_n = jnp.ones((SEQ, MLA_NOPE_DIM), F32)
    pad = jnp.zeros((SEQ, MLA_QK_PAD - MLA_QK_DIM), F32)
    cmq = jnp.tile(jnp.concatenate([ones_n, cm, cm, pad], axis=1) * scale, (1, MLA_HEADS))
    smq = jnp.tile(jnp.concatenate([0.0 * ones_n, sm, sm, pad], axis=1) * scale, (1, MLA_HEADS))
    ckp = jnp.concatenate([cm, cm, pad], axis=1)
    skp = jnp.concatenate([sm, sm, pad], axis=1)
    return cd, sd, cmq, smq, ckp, skp


def _constants():
    rd = _rot_matrix(512, [c * DIFF_HEAD_DIM for c in range(2 * DIFF_HEADS)], DIFF_ROT_DIM // 2)
    rmq = _rot_matrix(MLA_HEADS * MLA_QK_PAD, [h * MLA_QK_PAD + MLA_NOPE_DIM for h in range(MLA_HEADS)],
                      MLA_ROPE_DIM // 2)
    rkp = _rot_matrix(LANES, [0], MLA_ROPE_DIM // 2)
    tri = np.arange(CUM_T)
    u_incl = jnp.asarray(tri[:, None] <= tri[None, :], BF16)
    tri = np.arange(RANK_T)
    u_strict = jnp.asarray(tri[:, None] < tri[None, :], BF16)
    tri = np.arange(N_EXPERTS)
    l_incl = jnp.asarray(tri[:, None] >= tri[None, :], BF16)
    return (rd, rmq, rkp), u_incl, u_strict, l_incl


def kernel(x, w_in, b_fox_f, b_gate, diff_lambda, g_diff, g_mla_q, g_mla_kv, w_mla_uq, w_mla_ukv, w_fox_up, w_diff_up, w_mla_up, w_o, ln1_g, ln1_b, ln2_g, ln2_b, w_router, router_bias, w_exp_in, w_exp_out):
    s = x.shape[1]
    assert x.shape == (1, SEQ, D_MODEL) and w_in.shape == (DEPTH, D_MODEL, IN_WIDTH)
    xf = x.reshape(s, D_MODEL)
    tabs = _rope_tables()
    rmats, u_incl, u_strict, l_incl = _constants()
    w_a, w_s, w_g = _repack(w_in)

    perm = np.array([g * EXPERTS_PER_GROUP + j for j in range(EXPERTS_PER_GROUP) for g in range(N_GROUPS)])
    wr = jnp.concatenate(list(_split3(w_router[:, perm]))
                         + [jnp.zeros((D_MODEL, LANES - 3 * N_EXPERTS), BF16)], axis=1)
    rb = router_bias[perm].reshape(N_EXPERTS, 1)
    wgu = w_exp_in.astype(BF16)
    wout = w_exp_out.astype(BF16)
    zero_a = jnp.zeros((1, W_ATTN), F32)
    zero_s = jnp.zeros((1, LANES), F32)

    for i in range(DEPTH):
        pa = _linear(xf, w_a, i, zero_a, None, BF16, LIN_TN, "inproj_attn")
        ps = _linear(xf, w_s, i, zero_s, None, F32, LANES, "inproj_small")
        gates = _linear(xf, w_g, i, b_gate[i].reshape(1, -1), "sigmoid", BF16, LIN_TN, "inproj_gate")

        uq = w_mla_uq[i].reshape(MLA_Q_RANK, MLA_HEADS, MLA_QK_DIM)
        wuq = jnp.concatenate([uq, jnp.zeros((MLA_Q_RANK, MLA_HEADS, MLA_QK_PAD - MLA_QK_DIM), F32)], axis=2)
        wuq = wuq.reshape(MLA_Q_RANK, MLA_HEADS * MLA_QK_PAD).astype(BF16)
        ukv = w_mla_ukv[i].reshape(MLA_KV_RANK, MLA_HEADS, MLA_NOPE_DIM + MLA_V_DIM)
        wukv = jnp.concatenate([ukv[:, :, :MLA_NOPE_DIM].reshape(MLA_KV_RANK, -1),
                                ukv[:, :, MLA_NOPE_DIM:].reshape(MLA_KV_RANK, -1)], axis=1).astype(BF16)
        bf = jnp.zeros((1, LANES), F32).at[0, MLA_ROPE_DIM:MLA_ROPE_DIM + FOX_HEADS].set(b_fox_f[i])

        dq, dk, dvt, mq, mk, mvt, lft = _prep(pa, ps, tabs, g_mla_q[i].reshape(1, -1), g_mla_kv[i].reshape(1, -1),
                                              wuq, wukv, rmats, bf)
        cum = _cumsum(lft, u_incl)
        fq, fk, fvt = _foxprep(pa, cum)
        cum_row = cum[:FOX_HEADS].reshape(FOX_HEADS, 1, s)

        lam_init = 0.8 - 0.6 * math.exp(-0.3 * i)
        o_fox = _flash("fox", fq, fk, fvt, FOX_QK_PAD, FOX_HEADS, extra=(cum_row,))
        o_diff = _flash("diff", dq, dk, dvt, 2 * DIFF_HEAD_DIM, DIFF_HEADS,
                        extra=(diff_lambda[i], g_diff[i].reshape(1, -1)), lam_init=lam_init)
        o_mla = _flash("mla", mq, mk, mvt, MLA_QK_PAD, MLA_HEADS)

        x1, h16, eid, wt = _post(o_fox, o_diff, o_mla, gates, xf,
                                 w_fox_up[i].astype(BF16), w_diff_up[i].astype(BF16), w_mla_up[i].astype(BF16),
                                 w_o[i].astype(BF16), ln1_g[i].reshape(1, -1), ln1_b[i].reshape(1, -1), wr, rb)

        rank, cnt = _rank(eid, u_strict)
        dest, blk = _dest(cnt, eid, rank, l_incl)
        dest_flat = dest[:TOP_K].reshape(TOP_K * s)
        xs = _dispatch(dest_flat, h16, jnp.zeros((MOE_ROWS * ROW_CHUNKS, LANES), F32))
        y16 = _moe(blk[0], blk[1, :1], xs, wgu, wout, i)
        xf = _combine(dest_flat, y16, x1, wt, ln2_g[i].reshape(1, -1), ln2_b[i].reshape(1, -1))
    return xf.reshape(1, s, D_MODEL)
```

```python
import functools
import math

import numpy as np
import jax
import jax.numpy as jnp
from jax import lax
from jax.experimental import pallas as pl
from jax.experimental.pallas import tpu as pltpu

F32 = jnp.float32
BF16 = jnp.bfloat16

D_MODEL = 2048
SEQ = 16384
DEPTH = 2
ROPE_THETA = 500000.0
NORM_EPS = 1e-5
FOX_HEADS = 4
FOX_HEAD_DIM = 128
DIFF_HEADS = 4
DIFF_HEAD_DIM = 64
DIFF_V_DIM = 128
DIFF_ROT_DIM = 16
MLA_HEADS = 4
MLA_Q_RANK = 512
MLA_KV_RANK = 512
MLA_NOPE_DIM = 128
MLA_ROPE_DIM = 64
MLA_V_DIM = 128
MLA_QK_DIM = MLA_NOPE_DIM + MLA_ROPE_DIM
N_EXPERTS = 32
N_GROUPS = 8
EXPERTS_PER_GROUP = 4
TOP_K = 2
D_FF_EXPERT = 1408
DEEPNORM_ALPHA = (2 * DEPTH) ** 0.25

LANES = 128
ROW_CHUNKS = D_MODEL // LANES
MLA_QK_PAD = 256
FOX_QK_PAD = 256
BF16_ROWS = 16
VT_ROWS = LANES + BF16_ROWS
LOG2E = math.log2(math.e)
VMEM_LIMIT = 56 * 1024 * 1024

LIN_TM = 1024
LIN_TN = 1024
PREP_TM = 512
CUM_T = 512
ATT_T = 512
ATT_TQ = 2 * ATT_T
POST_TM = 256
RANK_T = 512
DEST_T = 2048
DISP_T = 512
MOE_R = 256
MOE_NB = (SEQ * TOP_K + N_EXPERTS * (MOE_R - 1) + MOE_R - 1) // MOE_R
MOE_ROWS = MOE_NB * MOE_R
BLK_LANES = 256
COMB_T = 256
ISSUE_UNROLL = 8
NEG = -1e30


def _cparams(n_axes, **kw):
    return pltpu.CompilerParams(dimension_semantics=("arbitrary",) * n_axes,
                                vmem_limit_bytes=VMEM_LIMIT, **kw)


def _dot(a, b):
    return jnp.dot(a, b, preferred_element_type=F32)


def _dot_nt(a, b):
    return lax.dot_general(a, b, (((1,), (1,)), ((), ())), preferred_element_type=F32)


def _split3(x):
    a = x.astype(BF16)
    r = x - a.astype(F32)
    b = r.astype(BF16)
    c = (r - b.astype(F32)).astype(BF16)
    return a, b, c


def _linear_kernel(x_ref, w_ref, b_ref, o_ref, xb_ref, *, act):
    @pl.when(pl.program_id(1) == 0)
    def _():
        xb_ref[...] = x_ref[...].astype(BF16)

    acc = _dot(xb_ref[...], w_ref[...]) + b_ref[...]
    if act == "sigmoid":
        acc = jax.nn.sigmoid(acc)
    o_ref[...] = acc.astype(o_ref.dtype)


def _linear(x, w, layer, b, act, out_dtype, tn, name):
    m, k = x.shape
    n = w.shape[2]
    return pl.pallas_call(
        functools.partial(_linear_kernel, act=act),
        grid=(m // LIN_TM, n // tn),
        in_specs=[pl.BlockSpec((LIN_TM, k), lambda i, j: (i, 0)),
                  pl.BlockSpec((None, k, tn), lambda i, j: (layer, 0, j)),
                  pl.BlockSpec((1, tn), lambda i, j: (0, j))],
        out_specs=pl.BlockSpec((LIN_TM, tn), lambda i, j: (i, j)),
        out_shape=jax.ShapeDtypeStruct((m, n), out_dtype),
        scratch_shapes=[pltpu.VMEM((LIN_TM, k), BF16)],
        compiler_params=_cparams(2),
        name=name,
    )(x, w, b)


O_FOX_F = 3 * 512
O_DIFF = O_FOX_F + FOX_HEADS
O_MLA = O_DIFF + 3 * 512
O_KPE = O_MLA + 2 * 512
O_GATE = O_KPE + MLA_ROPE_DIM
IN_WIDTH = O_GATE + 3 * D_MODEL
W_ATTN = 4096
REPACK_TK = 256


def _repack_kernel(w_ref, wa_ref, ws_ref, wg_ref):
    wa_ref[:, 0:O_FOX_F] = w_ref[:, 0:O_FOX_F].astype(BF16)
    wa_ref[:, O_FOX_F:W_ATTN] = w_ref[:, O_DIFF:O_KPE].astype(BF16)
    ws_ref[...] = jnp.zeros_like(ws_ref)
    ws_ref[:, 0:MLA_ROPE_DIM] = w_ref[:, O_KPE:O_GATE].astype(BF16)
    ws_ref[:, MLA_ROPE_DIM:MLA_ROPE_DIM + FOX_HEADS] = w_ref[:, O_FOX_F:O_DIFF].astype(BF16)
    wg_ref[...] = w_ref[:, O_GATE:IN_WIDTH].astype(BF16)


def _repack(w_in):
    k = w_in.shape[1]
    blk = lambda n: pl.BlockSpec((None, REPACK_TK, n), lambda l, i: (l, i, 0))
    return pl.pallas_call(
        _repack_kernel,
        grid=(DEPTH, k // REPACK_TK),
        in_specs=[blk(IN_WIDTH)],
        out_specs=[blk(W_ATTN), blk(LANES), blk(3 * D_MODEL)],
        out_shape=[jax.ShapeDtypeStruct((DEPTH, k, W_ATTN), BF16), jax.ShapeDtypeStruct((DEPTH, k, LANES), BF16),
                   jax.ShapeDtypeStruct((DEPTH, k, 3 * D_MODEL), BF16)],
        compiler_params=_cparams(2),
        name="repack",
    )(w_in)


def _rms(xf, g):
    ms = jnp.mean(jnp.square(xf), axis=-1, keepdims=True)
    return xf * lax.rsqrt(ms + NORM_EPS) * g


def _rope(x_f32, x_b16, r_ref, c, s):
    return x_f32 * c + _dot(x_b16, r_ref[...]) * s


def _store_vt(vt_ref, v, heads):
    vt = v.T.astype(BF16)
    ones = jnp.ones((BF16_ROWS, v.shape[0]), BF16)
    for h in range(heads):
        vt_ref[h * VT_ROWS:h * VT_ROWS + LANES, :] = vt[h * LANES:(h + 1) * LANES]
        vt_ref[h * VT_ROWS + LANES:(h + 1) * VT_ROWS, :] = ones


def _prep_kernel(dq_ref, dk_ref, dv_ref, cq_ref, ckv_ref, ps_ref, cd_ref, sd_ref, cmq_ref, smq_ref, ckp_ref, skp_ref,
                 gq_ref, gkv_ref, wuq_ref, wukv_ref, rd_ref, rmq_ref, rkp_ref, bf_ref,
                 dqo_ref, dko_ref, dvt_ref, mq_ref, mk_ref, mvt_ref, lft_ref):
    cd = cd_ref[...]
    sd = sd_ref[...]
    dq = dq_ref[...]
    dk = dk_ref[...]
    dqo_ref[...] = (_rope(dq.astype(F32), dq, rd_ref, cd, sd) * (DIFF_HEAD_DIM ** -0.5 * LOG2E)).astype(BF16)
    dko_ref[...] = _rope(dk.astype(F32), dk, rd_ref, cd, sd).astype(BF16)
    _store_vt(dvt_ref, dv_ref[...].astype(F32), DIFF_HEADS)

    cqn = _rms(cq_ref[...].astype(F32), gq_ref[...]).astype(BF16)
    q = _dot(cqn, wuq_ref[...])
    mq_ref[...] = _rope(q, q.astype(BF16), rmq_ref, cmq_ref[...], smq_ref[...]).astype(BF16)

    ckvn = _rms(ckv_ref[...].astype(F32), gkv_ref[...]).astype(BF16)
    kv = _dot(ckvn, wukv_ref[...])
    ps = ps_ref[...]
    kpe = _rope(ps, ps.astype(BF16), rkp_ref, ckp_ref[...], skp_ref[...]).astype(BF16)
    for h in range(MLA_HEADS):
        mk_ref[:, h * MLA_QK_PAD:h * MLA_QK_PAD + LANES] = kv[:, h * LANES:(h + 1) * LANES].astype(BF16)
        mk_ref[:, h * MLA_QK_PAD + LANES:(h + 1) * MLA_QK_PAD] = kpe
    _store_vt(mvt_ref, kv[:, MLA_HEADS * LANES:], MLA_HEADS)

    z = ps + bf_ref[...]
    lf = jnp.minimum(z, 0.0) - jnp.log1p(jnp.exp(-jnp.abs(z)))
    lft_ref[...] = lf.T[MLA_ROPE_DIM:MLA_ROPE_DIM + 8, :]


def _prep(pa, ps, tabs, gq, gkv, wuq, wukv, rmats, bf):
    s = pa.shape[0]
    tm = PREP_TM
    row = lambda w: pl.BlockSpec((tm, w), lambda i: (i, 0))
    col = lambda w, c: pl.BlockSpec((tm, w), lambda i: (i, c))
    tr = lambda r: pl.BlockSpec((r, tm), lambda i: (0, i))
    full = lambda a: pl.BlockSpec(a.shape, lambda i: (0,) * a.ndim)
    cd, sd, cmq, smq, ckp, skp = tabs
    rd, rmq, rkp = rmats
    return pl.pallas_call(
        _prep_kernel,
        grid=(s // tm,),
        in_specs=[col(512, 3), col(512, 4), col(512, 5), col(512, 6), col(512, 7), row(LANES),
                  row(512), row(512), row(1024), row(1024), row(LANES), row(LANES),
                  full(gq), full(gkv), full(wuq), full(wukv), full(rd), full(rmq), full(rkp), full(bf)],
        out_specs=[row(512), row(512), tr(4 * VT_ROWS), row(1024), row(1024), tr(4 * VT_ROWS), tr(8)],
        out_shape=[jax.ShapeDtypeStruct((s, 512), BF16), jax.ShapeDtypeStruct((s, 512), BF16),
                   jax.ShapeDtypeStruct((4 * VT_ROWS, s), BF16),
                   jax.ShapeDtypeStruct((s, 1024), BF16), jax.ShapeDtypeStruct((s, 1024), BF16),
                   jax.ShapeDtypeStruct((4 * VT_ROWS, s), BF16), jax.ShapeDtypeStruct((8, s), F32)],
        compiler_params=_cparams(1),
        name="prep",
    )(pa, pa, pa, pa, pa, ps, cd, sd, cmq, smq, ckp, skp, gq, gkv, wuq, wukv, rd, rmq, rkp, bf)


def _cumsum_kernel(x_ref, u_ref, o_ref, carry_ref):
    @pl.when(pl.program_id(0) == 0)
    def _():
        carry_ref[...] = jnp.zeros_like(carry_ref)

    a, b, c = _split3(x_ref[...])
    u = u_ref[...]
    cs = _dot(a, u) + _dot(b, u) + _dot(c, u) + carry_ref[:, :1]
    o_ref[...] = cs
    carry_ref[...] = jnp.broadcast_to(cs[:, CUM_T - 1:CUM_T], carry_ref.shape)


def _cumsum(x, u):
    s = x.shape[1]
    return pl.pallas_call(
        _cumsum_kernel,
        grid=(s // CUM_T,),
        in_specs=[pl.BlockSpec((8, CUM_T), lambda i: (0, i)), pl.BlockSpec(u.shape, lambda i: (0, 0))],
        out_specs=pl.BlockSpec((8, CUM_T), lambda i: (0, i)),
        out_shape=jax.ShapeDtypeStruct((8, s), F32),
        scratch_shapes=[pltpu.VMEM((8, LANES), F32)],
        compiler_params=_cparams(1),
        name="cumsum",
    )(x, u)


N_BIAS = 3


def _foxprep_kernel(q_ref, k_ref, v_ref, cum_ref, qo_ref, ko_ref, vt_ref):
    cum = cum_ref[...]
    beta = (cum[:, 0:1] - cum) * LOG2E
    pieces = [p.astype(F32).T for p in _split3(beta)]
    lane = lax.broadcasted_iota(jnp.int32, (ATT_T, LANES), 1)
    ones = jnp.where(lane < N_BIAS, 1.0, 0.0).astype(BF16)
    q = q_ref[...].astype(F32) * (FOX_HEAD_DIM ** -0.5 * LOG2E)
    for h in range(FOX_HEADS):
        e = jnp.zeros((ATT_T, LANES), F32)
        for n, p in enumerate(pieces):
            e = jnp.where(lane == n, p[:, h:h + 1], e)
        ko_ref[:, h * FOX_QK_PAD:h * FOX_QK_PAD + LANES] = k_ref[:, h * LANES:(h + 1) * LANES]
        ko_ref[:, h * FOX_QK_PAD + LANES:(h + 1) * FOX_QK_PAD] = e.astype(BF16)
        qo_ref[:, h * FOX_QK_PAD:h * FOX_QK_PAD + LANES] = q[:, h * LANES:(h + 1) * LANES].astype(BF16)
        qo_ref[:, h * FOX_QK_PAD + LANES:(h + 1) * FOX_QK_PAD] = ones
    _store_vt(vt_ref, v_ref[...].astype(F32), FOX_HEADS)


def _foxprep(pa, cum):
    s = pa.shape[0]
    t = ATT_T
    w = FOX_HEADS * FOX_QK_PAD
    return pl.pallas_call(
        _foxprep_kernel,
        grid=(s // t,),
        in_specs=[pl.BlockSpec((t, 512), lambda i: (i, 0)), pl.BlockSpec((t, 512), lambda i: (i, 1)),
                  pl.BlockSpec((t, 512), lambda i: (i, 2)), pl.BlockSpec((8, t), lambda i: (0, i))],
        out_specs=[pl.BlockSpec((t, w), lambda i: (i, 0)), pl.BlockSpec((t, w), lambda i: (i, 0)),
                   pl.BlockSpec((FOX_HEADS * VT_ROWS, t), lambda i: (0, i))],
        out_shape=[jax.ShapeDtypeStruct((s, w), BF16), jax.ShapeDtypeStruct((s, w), BF16),
                   jax.ShapeDtypeStruct((FOX_HEADS * VT_ROWS, s), BF16)],
        compiler_params=_cparams(1),
        name="foxprep",
    )(pa, pa, pa, cum)


def _flash_kernel(*refs, mode, lam_init):
    t = ATT_T
    if mode == "fox":
        q_ref, k_ref, vt_ref, ck_ref, o_ref, sa_ref, sb_ref, xa_ref, xb_ref, m_ref, acc_ref = refs
    elif mode == "diff":
        q_ref, k_ref, vt_ref, lam_ref, gd_ref, o_ref, sa_ref, sb_ref, xa_ref, xb_ref, m_ref, acc_ref = refs
    else:
        q_ref, k_ref, vt_ref, o_ref, sa_ref, sb_ref, xa_ref, xb_ref, m_ref, acc_ref = refs
    i = pl.program_id(1)
    off_i = pl.multiple_of(i * ATT_TQ, ATT_TQ)
    m_ref[...] = jnp.full(m_ref.shape, NEG, F32)
    acc_ref[...] = jnp.zeros_like(acc_ref)

    q = q_ref[...]
    if mode == "diff":
        lane = lax.broadcasted_iota(jnp.int32, q.shape, 1)
        zero = jnp.zeros_like(q)
        qs = [jnp.where(lane < DIFF_HEAD_DIM, q, zero), jnp.where(lane >= DIFF_HEAD_DIM, q, zero)]
    else:
        qs = [q]

    buf_a = (sa_ref, xa_ref)
    buf_b = (sb_ref, xb_ref)

    def scores(off, buf):
        s_ref, x_ref = buf
        kb = k_ref[pl.ds(off, t), :]
        for n, qn in enumerate(qs):
            st = _dot_nt(kb, qn)
            s_ref[n] = st
            x_ref[n] = jnp.max(st, axis=0, keepdims=True)

    def consume(off, buf, masked):
        s_ref, x_ref = buf
        vtb = vt_ref[:, pl.ds(off, t)]
        for n in range(len(qs)):
            st = s_ref[n]
            if masked:
                key_id = lax.broadcasted_iota(jnp.int32, st.shape, 0) + (off - off_i)
                qry_id = lax.broadcasted_iota(jnp.int32, st.shape, 1)
                st = jnp.where(key_id <= qry_id, st, NEG)
                m = jnp.max(st, axis=0, keepdims=True)
            else:
                m = x_ref[n]
            p = jnp.exp2((st - m).astype(BF16))
            o = _dot(vtb, p)
            if mode == "fox":
                m = m + (ck_ref[:, pl.ds(off_i, LANES)][:, :1] - ck_ref[:, pl.ds(off, LANES)][:, :1]) * LOG2E
            m_prev = m_ref[n]
            m_new = jnp.maximum(m_prev, m)
            acc_ref[n] = jnp.exp2(m_prev - m_new) * acc_ref[n] + jnp.exp2(m - m_new) * o
            m_ref[n] = m_new

    scores(0, buf_a)

    def pair(jj, carry):
        off0 = pl.multiple_of(jj * (2 * t), 2 * t)
        off1 = pl.multiple_of(off0 + t, t)
        scores(off1, buf_b)
        consume(off0, buf_a, False)
        scores(pl.multiple_of(off1 + t, t), buf_a)
        consume(off1, buf_b, False)
        return carry

    lax.fori_loop(0, i, pair, 0)
    off_d = pl.multiple_of(off_i + t, t)
    scores(off_d, buf_b)
    consume(off_i, buf_a, True)
    consume(off_d, buf_b, True)

    def normalised(n):
        acc = acc_ref[n]
        return acc[:LANES] / acc[LANES:LANES + 1]

    if mode == "diff":
        dl = lam_ref[...]
        lam = (jnp.exp(jnp.sum(dl[0:1] * dl[1:2], axis=1, keepdims=True))
               - jnp.exp(jnp.sum(dl[2:3] * dl[3:4], axis=1, keepdims=True)) + lam_init)
        o = _rms((normalised(0) - lam * normalised(1)).T, gd_ref[...]) * (1.0 - lam_init)
    else:
        o = normalised(0).T
    o_ref[...] = o.astype(o_ref.dtype)


def _flash(mode, q_arr, k_arr, vt_arr, dk, heads, extra=(), lam_init=0.0):
    s = q_arr.shape[0]
    t = ATT_T
    tq = ATT_TQ
    dv = LANES
    n_stream = 2 if mode == "diff" else 1
    in_specs = [pl.BlockSpec((tq, dk), lambda h, i: (i, h)),
                pl.BlockSpec((s, dk), lambda h, i: (0, h)),
                pl.BlockSpec((VT_ROWS, s), lambda h, i: (h, 0))]
    if mode == "fox":
        in_specs.append(pl.BlockSpec((None, 1, s), lambda h, i: (h, 0, 0)))
    elif mode == "diff":
        in_specs += [pl.BlockSpec(extra[0].shape, lambda h, i: (0, 0)),
                     pl.BlockSpec(extra[1].shape, lambda h, i: (0, 0))]
    return pl.pallas_call(
        functools.partial(_flash_kernel, mode=mode, lam_init=lam_init),
        grid=(heads, s // tq),
        in_specs=in_specs,
        out_specs=pl.BlockSpec((tq, dv), lambda h, i: (i, h)),
        out_shape=jax.ShapeDtypeStruct((s, heads * dv), BF16),
        scratch_shapes=[pltpu.VMEM((n_stream, t, tq), F32), pltpu.VMEM((n_stream, t, tq), F32),
                        pltpu.VMEM((n_stream, 1, tq), F32), pltpu.VMEM((n_stream, 1, tq), F32),
                        pltpu.VMEM((n_stream, 1, tq), F32), pltpu.VMEM((n_stream, VT_ROWS, tq), F32)],
        compiler_params=_cparams(2),
        name="flash_" + mode,
    )(q_arr, k_arr, vt_arr, *extra)


def _layer_norm(z, g, b):
    mu = jnp.mean(z, axis=-1, keepdims=True)
    zc = z - mu
    var = jnp.mean(jnp.square(zc), axis=-1, keepdims=True)
    return zc * lax.rsqrt(var + NORM_EPS) * g + b


def _first_index_of_max(vals):
    best = vals[0]
    for v in vals[1:]:
        best = jnp.maximum(best, v)
    idx = jnp.full(best.shape, len(vals) - 1, jnp.int32)
    for j in range(len(vals) - 2, -1, -1):
        idx = jnp.where(vals[j] == best, j, idx)
    return best, idx


def _post_kernel(of_ref, od_ref, om_ref, g0_ref, g1_ref, g2_ref, x_ref, wf_ref, wd_ref, wm_ref, wo_ref,
                 lg_ref, lb_ref, wr_ref, rb_ref, x1_ref, h16_ref, eid_ref, wt_ref):
    tm = POST_TM
    merged = (g0_ref[...].astype(F32) * _dot(of_ref[...], wf_ref[...])
              + g1_ref[...].astype(F32) * _dot(od_ref[...], wd_ref[...])
              + g2_ref[...].astype(F32) * _dot(om_ref[...], wm_ref[...]))
    mix = _dot(merged.astype(BF16), wo_ref[...])
    x1 = _layer_norm(DEEPNORM_ALPHA * x_ref[...] + mix, lg_ref[...], lb_ref[...])
    x1_ref[...] = x1
    for c in range(ROW_CHUNKS):
        h16_ref[pl.ds(c, tm, stride=ROW_CHUNKS), :] = x1[:, c * LANES:(c + 1) * LANES]

    e = N_EXPERTS
    prod = [_dot(p, wr_ref[...]).T for p in _split3(x1)]
    logits = (prod[0][0:e] + (prod[0][e:2 * e] + prod[1][0:e])
              + (prod[0][2 * e:3 * e] + prod[1][e:2 * e] + prod[2][0:e]))
    scores = jax.nn.sigmoid(logits)
    biased = scores + rb_ref[...]
    a = [biased[j * 8:(j + 1) * 8] for j in range(EXPERTS_PER_GROUP)]
    u = [scores[j * 8:(j + 1) * 8] for j in range(EXPERTS_PER_GROUP)]
    gs = a[0] + a[1]
    for j0 in range(EXPERTS_PER_GROUP):
        for j1 in range(j0 + 1, EXPERTS_PER_GROUP):
            gs = jnp.maximum(gs, a[j0] + a[j1])
    gmax = jnp.max(gs, axis=0, keepdims=True)
    gid = lax.broadcasted_iota(jnp.int32, gs.shape, 0)
    best = jnp.min(jnp.where(gs == gmax, gid, N_GROUPS), axis=0, keepdims=True)
    sel = gid == best
    ing = [jnp.sum(jnp.where(sel, a[j], 0.0), axis=0, keepdims=True) for j in range(EXPERTS_PER_GROUP)]
    unb = [jnp.sum(jnp.where(sel, u[j], 0.0), axis=0, keepdims=True) for j in range(EXPERTS_PER_GROUP)]
    _, l1 = _first_index_of_max(ing)
    ing2 = [jnp.where(l1 == j, -jnp.inf, ing[j]) for j in range(EXPERTS_PER_GROUP)]
    _, l2 = _first_index_of_max(ing2)
    w1 = jnp.zeros_like(unb[0])
    w2 = jnp.zeros_like(unb[0])
    for j in range(EXPERTS_PER_GROUP):
        w1 = jnp.where(l1 == j, unb[j], w1)
        w2 = jnp.where(l2 == j, unb[j], w2)
    tot = w1 + w2
    eid_ref[...] = jnp.zeros_like(eid_ref)
    wt_ref[...] = jnp.zeros_like(wt_ref)
    eid_ref[0:1, :] = best * EXPERTS_PER_GROUP + l1
    eid_ref[1:2, :] = best * EXPERTS_PER_GROUP + l2
    wt_ref[0:1, :] = w1 / tot
    wt_ref[1:2, :] = w2 / tot


def _post(o_fox, o_diff, o_mla, gates, x, wf, wd, wm, wo, lg, lb, wr, rb):
    s = x.shape[0]
    tm = POST_TM
    row = lambda w: pl.BlockSpec((tm, w), lambda i: (i, 0))
    full = lambda a: pl.BlockSpec(a.shape, lambda i: (0,) * a.ndim, pipeline_mode=pl.Buffered(1))
    return pl.pallas_call(
        _post_kernel,
        grid=(s // tm,),
        in_specs=[row(512), row(512), row(512),
                  pl.BlockSpec((tm, D_MODEL), lambda i: (i, 0)),
                  pl.BlockSpec((tm, D_MODEL), lambda i: (i, 1)),
                  pl.BlockSpec((tm, D_MODEL), lambda i: (i, 2)),
                  row(D_MODEL), full(wf), full(wd), full(wm), full(wo), full(lg), full(lb), full(wr), full(rb)],
        out_specs=[row(D_MODEL), pl.BlockSpec((tm * ROW_CHUNKS, LANES), lambda i: (i, 0)),
                   pl.BlockSpec((8, tm), lambda i: (0, i)), pl.BlockSpec((8, tm), lambda i: (0, i))],
        out_shape=[jax.ShapeDtypeStruct((s, D_MODEL), F32),
                   jax.ShapeDtypeStruct((s * ROW_CHUNKS, LANES), F32),
                   jax.ShapeDtypeStruct((8, s), jnp.int32), jax.ShapeDtypeStruct((8, s), F32)],
        compiler_params=_cparams(1),
        name="post",
    )(o_fox, o_diff, o_mla, gates, gates, gates, x, wf, wd, wm, wo, lg, lb, wr, rb)


def _rank_kernel(eid_ref, us_ref, rank_ref, cnt_ref, carry_ref):
    @pl.when(pl.program_id(0) == 0)
    def _():
        carry_ref[...] = jnp.zeros_like(carry_ref)

    e0 = eid_ref[0:1, :]
    e1 = eid_ref[1:2, :]
    eio = lax.broadcasted_iota(jnp.int32, (N_EXPERTS, RANK_T), 0)
    hit0 = eio == e0
    hit1 = eio == e1
    onehot = jnp.where(hit0 | hit1, 1.0, 0.0)
    before = _dot(onehot.astype(BF16), us_ref[...]) + carry_ref[:, :1]
    rank_ref[...] = jnp.zeros_like(rank_ref)
    rank_ref[0:1, :] = jnp.sum(jnp.where(hit0, before, 0.0), axis=0, keepdims=True).astype(jnp.int32)
    rank_ref[1:2, :] = jnp.sum(jnp.where(hit1, before, 0.0), axis=0, keepdims=True).astype(jnp.int32)
    carry_ref[...] = carry_ref[...] + jnp.sum(onehot, axis=1, keepdims=True)
    cnt_ref[...] = carry_ref[...]


def _rank(eid, us):
    s = eid.shape[1]
    return pl.pallas_call(
        _rank_kernel,
        grid=(s // RANK_T,),
        in_specs=[pl.BlockSpec((8, RANK_T), lambda i: (0, i)), pl.BlockSpec(us.shape, lambda i: (0, 0))],
        out_specs=[pl.BlockSpec((8, RANK_T), lambda i: (0, i)),
                   pl.BlockSpec((N_EXPERTS, LANES), lambda i: (0, 0))],
        out_shape=[jax.ShapeDtypeStruct((8, s), jnp.int32), jax.ShapeDtypeStruct((N_EXPERTS, LANES), F32)],
        scratch_shapes=[pltpu.VMEM((N_EXPERTS, LANES), F32)],
        compiler_params=_cparams(1),
        name="rank",
    )(eid, us)


def _dest_kernel(cnt_ref, eid_ref, rank_ref, lt_ref, dest_ref, blk_ref):
    nb = jnp.floor((cnt_ref[...] + (MOE_R - 1)) * (1.0 / MOE_R))
    incl = _dot(lt_ref[...], nb.astype(BF16))
    start = (incl - nb)[:, :1] * MOE_R
    eio = lax.broadcasted_iota(jnp.int32, (N_EXPERTS, DEST_T), 0)
    dest_ref[...] = jnp.zeros_like(dest_ref)
    for k in range(TOP_K):
        base = jnp.sum(jnp.where(eio == eid_ref[k:k + 1, :], start, 0.0), axis=0, keepdims=True)
        dest_ref[k:k + 1, :] = base.astype(jnp.int32) + rank_ref[k:k + 1, :]
    bid = lax.broadcasted_iota(jnp.int32, (N_EXPERTS, BLK_LANES), 1).astype(F32)
    ended = jnp.sum(jnp.where(incl[:, :1] <= bid, 1.0, 0.0), axis=0, keepdims=True)
    blk_ref[...] = jnp.zeros_like(blk_ref)
    blk_ref[0:1, :] = jnp.minimum(ended, N_EXPERTS - 1.0).astype(jnp.int32)
    blk_ref[1:2, :] = jnp.broadcast_to(incl[N_EXPERTS - 1:N_EXPERTS, :1], (1, BLK_LANES)).astype(jnp.int32)


def _dest(cnt, eid, rank, lt):
    s = eid.shape[1]
    return pl.pallas_call(
        _dest_kernel,
        grid=(s // DEST_T,),
        in_specs=[pl.BlockSpec(cnt.shape, lambda i: (0, 0)),
                  pl.BlockSpec((8, DEST_T), lambda i: (0, i)), pl.BlockSpec((8, DEST_T), lambda i: (0, i)),
                  pl.BlockSpec(lt.shape, lambda i: (0, 0))],
        out_specs=[pl.BlockSpec((8, DEST_T), lambda i: (0, i)), pl.BlockSpec((8, BLK_LANES), lambda i: (0, 0))],
        out_shape=[jax.ShapeDtypeStruct((8, s), jnp.int32), jax.ShapeDtypeStruct((8, BLK_LANES), jnp.int32)],
        compiler_params=_cparams(1),
        name="dest",
    )(cnt, eid, rank, lt)


def _row_copy(src_ref, src_row, dst_ref, dst_row, sem):
    def rows(row):
        start = row * ROW_CHUNKS
        return pl.ds(start if isinstance(row, int) else pl.multiple_of(start, ROW_CHUNKS), ROW_CHUNKS)

    return pltpu.make_async_copy(src_ref.at[rows(src_row)], dst_ref.at[rows(dst_row)], sem)


def _dispatch_kernel(dest_ref, h_ref, xs_in_ref, xs_ref, sem):
    del xs_in_ref
    base = pl.program_id(0) * DISP_T

    def issue(r, carry):
        for k in range(TOP_K):
            _row_copy(h_ref, r, xs_ref, dest_ref[k * SEQ + base + r], sem).start(priority=k)
        return carry

    lax.fori_loop(0, DISP_T, issue, 0, unroll=ISSUE_UNROLL)
    for k in range(TOP_K):
        pltpu.make_async_copy(h_ref, xs_ref.at[pl.ds(0, DISP_T * ROW_CHUNKS)], sem).wait()


def _dispatch(dest_flat, h16, xs_zero):
    return pl.pallas_call(
        _dispatch_kernel,
        grid_spec=pltpu.PrefetchScalarGridSpec(
            num_scalar_prefetch=1, grid=(SEQ // DISP_T,),
            in_specs=[pl.BlockSpec((DISP_T * ROW_CHUNKS, LANES), lambda i, d: (i, 0)),
                      pl.BlockSpec(memory_space=pl.ANY)],
            out_specs=pl.BlockSpec(memory_space=pl.ANY),
            scratch_shapes=[pltpu.SemaphoreType.DMA(())]),
        out_shape=jax.ShapeDtypeStruct(xs_zero.shape, xs_zero.dtype),
        input_output_aliases={2: 0},
        compiler_params=_cparams(1),
        name="dispatch",
    )(dest_flat, h16, xs_zero)


def _moe_kernel(be_ref, nu_ref, xs_ref, wgu_ref, wout_ref, y_ref, lhs_ref):
    del be_ref
    used = pl.program_id(0) < nu_ref[0]

    @pl.when(jnp.logical_not(used))
    def _():
        y_ref[...] = jnp.zeros_like(y_ref)

    @pl.when(used)
    def _():
        for c in range(ROW_CHUNKS):
            lhs_ref[:, c * LANES:(c + 1) * LANES] = xs_ref[pl.ds(c, MOE_R, stride=ROW_CHUNKS), :].astype(BF16)
        gu = _dot(lhs_ref[...], wgu_ref[...])
        hmid = (jax.nn.silu(gu[:, :D_FF_EXPERT]) * gu[:, D_FF_EXPERT:]).astype(BF16)
        y = _dot(hmid, wout_ref[...])
        for c in range(ROW_CHUNKS):
            y_ref[pl.ds(c, MOE_R, stride=ROW_CHUNKS), :] = y[:, c * LANES:(c + 1) * LANES]


def _moe(blk_e, n_used, xs, wgu, wout, layer):
    last = lambda b, nu: jnp.minimum(b, nu[0] - 1)
    rows = pl.BlockSpec((MOE_R * ROW_CHUNKS, LANES), lambda b, be, nu: (last(b, nu), 0))
    return pl.pallas_call(
        _moe_kernel,
        grid_spec=pltpu.PrefetchScalarGridSpec(
            num_scalar_prefetch=2, grid=(MOE_NB,),
            in_specs=[rows,
                      pl.BlockSpec((None, None) + wgu.shape[2:], lambda b, be, nu: (layer, be[last(b, nu)], 0, 0)),
                      pl.BlockSpec((None, None) + wout.shape[2:], lambda b, be, nu: (layer, be[last(b, nu)], 0, 0))],
            out_specs=pl.BlockSpec((MOE_R * ROW_CHUNKS, LANES), lambda b, be, nu: (b, 0)),
            scratch_shapes=[pltpu.VMEM((MOE_R, D_MODEL), BF16)]),
        out_shape=jax.ShapeDtypeStruct(xs.shape, F32),
        compiler_params=_cparams(1),
        name="moe",
    )(blk_e, n_used, xs, wgu, wout)


def _combine_kernel(dest_ref, y_ref, x1_ref, wt_ref, g_ref, b_ref, o_ref, buf_ref, sem):
    i = pl.program_id(0)
    slot = i & 1

    def issue(tile, sl):
        def body(r, carry):
            for k in range(TOP_K):
                _row_copy(y_ref, dest_ref[k * SEQ + tile * COMB_T + r], buf_ref.at[sl, k], r,
                          sem.at[sl]).start(priority=k)
            return carry

        lax.fori_loop(0, COMB_T, body, 0, unroll=ISSUE_UNROLL)

    @pl.when(i == 0)
    def _():
        issue(0, 0)

    @pl.when(i + 1 < pl.num_programs(0))
    def _():
        issue(i + 1, 1 - slot)

    for k in range(TOP_K):
        pltpu.make_async_copy(y_ref.at[pl.ds(0, COMB_T * ROW_CHUNKS)], buf_ref.at[slot, k], sem.at[slot]).wait()

    wcol = wt_ref[...].T
    w0 = wcol[:, 0:1]
    w1 = wcol[:, 1:2]
    ffn = jnp.concatenate(
        [w0 * buf_ref[slot, 0, pl.ds(c, COMB_T, stride=ROW_CHUNKS), :]
         + w1 * buf_ref[slot, 1, pl.ds(c, COMB_T, stride=ROW_CHUNKS), :] for c in range(ROW_CHUNKS)], axis=1)
    o_ref[...] = _layer_norm(DEEPNORM_ALPHA * x1_ref[...] + ffn, g_ref[...], b_ref[...])


def _combine(dest_flat, y16, x1, wt, g, b):
    tm = COMB_T
    return pl.pallas_call(
        _combine_kernel,
        grid_spec=pltpu.PrefetchScalarGridSpec(
            num_scalar_prefetch=1, grid=(SEQ // tm,),
            in_specs=[pl.BlockSpec(memory_space=pl.ANY),
                      pl.BlockSpec((tm, D_MODEL), lambda i, d: (i, 0)),
                      pl.BlockSpec((8, tm), lambda i, d: (0, i)),
                      pl.BlockSpec(g.shape, lambda i, d: (0, 0)), pl.BlockSpec(b.shape, lambda i, d: (0, 0))],
            out_specs=pl.BlockSpec((tm, D_MODEL), lambda i, d: (i, 0)),
            scratch_shapes=[pltpu.VMEM((2, TOP_K, tm * ROW_CHUNKS, LANES), F32), pltpu.SemaphoreType.DMA((2,))]),
        out_shape=jax.ShapeDtypeStruct(x1.shape, F32),
        compiler_params=_cparams(1),
        name="combine",
    )(dest_flat, y16, x1, wt, g, b)


def _rot_matrix(width, groups, half):
    r = np.zeros((width, width), np.float32)
    for o in groups:
        for j in range(half):
            r[o + j + half, o + j] = -1.0
            r[o + j, o + half + j] = 1.0
    return jnp.asarray(r, BF16)


def _rope_tables():
    pos = jnp.arange(SEQ, dtype=F32)

    def cs(dim):
        inv_freq = ROPE_THETA ** (-jnp.arange(0, dim, 2, dtype=F32) / dim)
        ang = pos[:, None] * inv_freq[None, :]
        return jnp.cos(ang), jnp.sin(ang)

    cp, sp = cs(DIFF_ROT_DIM)
    one = jnp.ones((SEQ, DIFF_HEAD_DIM - DIFF_ROT_DIM), F32)
    cd = jnp.tile(jnp.concatenate([cp, cp, one], axis=1), (1, 2 * DIFF_HEADS))
    sd = jnp.tile(jnp.concatenate([sp, sp, 0.0 * one], axis=1), (1, 2 * DIFF_HEADS))
    cm, sm = cs(MLA_ROPE_DIM)
    scale = MLA_QK_DIM ** -0.5 * LOG2E
    ones_n = jnp.ones((SEQ, MLA_NOPE_DIM), F32)
    pad = jnp.zeros((SEQ, MLA_QK_PAD - MLA_QK_DIM), F32)
    cmq = jnp.tile(jnp.concatenate([ones_n, cm, cm, pad], axis=1) * scale, (1, MLA_HEADS))
    smq = jnp.tile(jnp.concatenate([0.0 * ones_n, sm, sm, pad], axis=1) * scale, (1, MLA_HEADS))
    ckp = jnp.concatenate([cm, cm, pad], axis=1)
    skp = jnp.concatenate([sm, sm, pad], axis=1)
    return cd, sd, cmq, smq, ckp, skp


def _constants():
    rd = _rot_matrix(512, [c * DIFF_HEAD_DIM for c in range(2 * DIFF_HEADS)], DIFF_ROT_DIM // 2)
    rmq = _rot_matrix(MLA_HEADS * MLA_QK_PAD, [h * MLA_QK_PAD + MLA_NOPE_DIM for h in range(MLA_HEADS)],
                      MLA_ROPE_DIM // 2)
    rkp = _rot_matrix(LANES, [0], MLA_ROPE_DIM // 2)
    tri = np.arange(CUM_T)
    u_incl = jnp.asarray(tri[:, None] <= tri[None, :], BF16)
    tri = np.arange(RANK_T)
    u_strict = jnp.asarray(tri[:, None] < tri[None, :], BF16)
    tri = np.arange(N_EXPERTS)
    l_incl = jnp.asarray(tri[:, None] >= tri[None, :], BF16)
    return (rd, rmq, rkp), u_incl, u_strict, l_incl


def kernel(x, w_in, b_fox_f, b_gate, diff_lambda, g_diff, g_mla_q, g_mla_kv, w_mla_uq, w_mla_ukv, w_fox_up, w_diff_up, w_mla_up, w_o, ln1_g, ln1_b, ln2_g, ln2_b, w_router, router_bias, w_exp_in, w_exp_out):
    s = x.shape[1]
    assert x.shape == (1, SEQ, D_MODEL) and w_in.shape == (DEPTH, D_MODEL, IN_WIDTH)
    xf = x.reshape(s, D_MODEL)
    tabs = _rope_tables()
    rmats, u_incl, u_strict, l_incl = _constants()
    w_a, w_s, w_g = _repack(w_in)

    perm = np.array([g * EXPERTS_PER_GROUP + j for j in range(EXPERTS_PER_GROUP) for g in range(N_GROUPS)])
    wr = jnp.concatenate(list(_split3(w_router[:, perm]))
                         + [jnp.zeros((D_MODEL, LANES - 3 * N_EXPERTS), BF16)], axis=1)
    rb = router_bias[perm].reshape(N_EXPERTS, 1)
    wgu = w_exp_in.astype(BF16)
    wout = w_exp_out.astype(BF16)
    zero_a = jnp.zeros((1, W_ATTN), F32)
    zero_s = jnp.zeros((1, LANES), F32)

    for i in range(DEPTH):
        pa = _linear(xf, w_a, i, zero_a, None, BF16, LIN_TN, "inproj_attn")
        ps = _linear(xf, w_s, i, zero_s, None, F32, LANES, "inproj_small")
        gates = _linear(xf, w_g, i, b_gate[i].reshape(1, -1), "sigmoid", BF16, LIN_TN, "inproj_gate")

        uq = w_mla_uq[i].reshape(MLA_Q_RANK, MLA_HEADS, MLA_QK_DIM)
        wuq = jnp.concatenate([uq, jnp.zeros((MLA_Q_RANK, MLA_HEADS, MLA_QK_PAD - MLA_QK_DIM), F32)], axis=2)
        wuq = wuq.reshape(MLA_Q_RANK, MLA_HEADS * MLA_QK_PAD).astype(BF16)
        ukv = w_mla_ukv[i].reshape(MLA_KV_RANK, MLA_HEADS, MLA_NOPE_DIM + MLA_V_DIM)
        wukv = jnp.concatenate([ukv[:, :, :MLA_NOPE_DIM].reshape(MLA_KV_RANK, -1),
                                ukv[:, :, MLA_NOPE_DIM:].reshape(MLA_KV_RANK, -1)], axis=1).astype(BF16)
        bf = jnp.zeros((1, LANES), F32).at[0, MLA_ROPE_DIM:MLA_ROPE_DIM + FOX_HEADS].set(b_fox_f[i])

        dq, dk, dvt, mq, mk, mvt, lft = _prep(pa, ps, tabs, g_mla_q[i].reshape(1, -1), g_mla_kv[i].reshape(1, -1),
                                              wuq, wukv, rmats, bf)
        cum = _cumsum(lft, u_incl)
        fq, fk, fvt = _foxprep(pa, cum)
        cum_row = cum[:FOX_HEADS].reshape(FOX_HEADS, 1, s)

        lam_init = 0.8 - 0.6 * math.exp(-0.3 * i)
        o_fox = _flash("fox", fq, fk, fvt, FOX_QK_PAD, FOX_HEADS, extra=(cum_row,))
        o_diff = _flash("diff", dq, dk, dvt, 2 * DIFF_HEAD_DIM, DIFF_HEADS,
                        extra=(diff_lambda[i], g_diff[i].reshape(1, -1)), lam_init=lam_init)
        o_mla = _flash("mla", mq, mk, mvt, MLA_QK_PAD, MLA_HEADS)

        x1, h16, eid, wt = _post(o_fox, o_diff, o_mla, gates, xf,
                                 w_fox_up[i].astype(BF16), w_diff_up[i].astype(BF16), w_mla_up[i].astype(BF16),
                                 w_o[i].astype(BF16), ln1_g[i].reshape(1, -1), ln1_b[i].reshape(1, -1), wr, rb)

        rank, cnt = _rank(eid, u_strict)
        dest, blk = _dest(cnt, eid, rank, l_incl)
        dest_flat = dest[:TOP_K].reshape(TOP_K * s)
        xs = _dispatch(dest_flat, h16, jnp.zeros((MOE_ROWS * ROW_CHUNKS, LANES), F32))
        y16 = _moe(blk[0], blk[1, :1], xs, wgu, wout, i)
        xf = _combine(dest_flat, y16, x1, wt, ln2_g[i].reshape(1, -1), ln2_b[i].reshape(1, -1))
    return xf.reshape(1, s, D_MODEL)
```

```python
import functools
import math

import numpy as np
import jax
import jax.numpy as jnp
from jax import lax
from jax.experimental import pallas as pl
from jax.experimental.pallas import tpu as pltpu

F32 = jnp.float32
BF16 = jnp.bfloat16

D_MODEL = 2048
SEQ = 16384
DEPTH = 2
ROPE_THETA = 500000.0
NORM_EPS = 1e-5
FOX_HEADS = 4
FOX_HEAD_DIM = 128
DIFF_HEADS = 4
DIFF_HEAD_DIM = 64
DIFF_V_DIM = 128
DIFF_ROT_DIM = 16
MLA_HEADS = 4
MLA_Q_RANK = 512
MLA_KV_RANK = 512
MLA_NOPE_DIM = 128
MLA_ROPE_DIM = 64
MLA_V_DIM = 128
MLA_QK_DIM = MLA_NOPE_DIM + MLA_ROPE_DIM
N_EXPERTS = 32
N_GROUPS = 8
EXPERTS_PER_GROUP = 4
TOP_K = 2
D_FF_EXPERT = 1408
DEEPNORM_ALPHA = (2 * DEPTH) ** 0.25

LANES = 128
ROW_CHUNKS = D_MODEL // LANES
MLA_QK_PAD = 256
FOX_QK_PAD = 256
BF16_ROWS = 16
VT_ROWS = LANES + BF16_ROWS
LOG2E = math.log2(math.e)
VMEM_LIMIT = 56 * 1024 * 1024

LIN_TM = 1024
LIN_TN = 1024
PREP_TM = 512
CUM_T = 512
ATT_T = 512
ATT_TQ = 2 * ATT_T
POST_TM = 256
RANK_T = 512
DEST_T = 2048
DISP_T = 512
MOE_R = 256
MOE_NB = (SEQ * TOP_K + N_EXPERTS * (MOE_R - 1) + MOE_R - 1) // MOE_R
MOE_ROWS = MOE_NB * MOE_R
BLK_LANES = 256
COMB_T = 256
ISSUE_UNROLL = 8
NEG = -1e30


def _cparams(n_axes, **kw):
    return pltpu.CompilerParams(dimension_semantics=("arbitrary",) * n_axes,
                                vmem_limit_bytes=VMEM_LIMIT, **kw)


def _dot(a, b):
    return jnp.dot(a, b, preferred_element_type=F32)


def _dot_nt(a, b):
    return lax.dot_general(a, b, (((1,), (1,)), ((), ())), preferred_element_type=F32)


def _split3(x):
    a = x.astype(BF16)
    r = x - a.astype(F32)
    b = r.astype(BF16)
    c = (r - b.astype(F32)).astype(BF16)
    return a, b, c


def _linear_kernel(x_ref, w_ref, b_ref, o_ref, xb_ref, *, act):
    @pl.when(pl.program_id(1) == 0)
    def _():
        xb_ref[...] = x_ref[...].astype(BF16)

    acc = _dot(xb_ref[...], w_ref[...]) + b_ref[...]
    if act == "sigmoid":
        acc = jax.nn.sigmoid(acc)
    o_ref[...] = acc.astype(o_ref.dtype)


def _linear(x, w, layer, b, act, out_dtype, tn, name):
    m, k = x.shape
    n = w.shape[2]
    return pl.pallas_call(
        functools.partial(_linear_kernel, act=act),
        grid=(m // LIN_TM, n // tn),
        in_specs=[pl.BlockSpec((LIN_TM, k), lambda i, j: (i, 0)),
                  pl.BlockSpec((None, k, tn), lambda i, j: (layer, 0, j)),
                  pl.BlockSpec((1, tn), lambda i, j: (0, j))],
        out_specs=pl.BlockSpec((LIN_TM, tn), lambda i, j: (i, j)),
        out_shape=jax.ShapeDtypeStruct((m, n), out_dtype),
        scratch_shapes=[pltpu.VMEM((LIN_TM, k), BF16)],
        compiler_params=_cparams(2),
        name=name,
    )(x, w, b)


O_FOX_F = 3 * 512
O_DIFF = O_FOX_F + FOX_HEADS
O_MLA = O_DIFF + 3 * 512
O_KPE = O_MLA + 2 * 512
O_GATE = O_KPE + MLA_ROPE_DIM
IN_WIDTH = O_GATE + 3 * D_MODEL
W_ATTN = 4096
REPACK_TK = 256


def _repack_kernel(w_ref, wa_ref, ws_ref, wg_ref):
    wa_ref[:, 0:O_FOX_F] = w_ref[:, 0:O_FOX_F].astype(BF16)
    wa_ref[:, O_FOX_F:W_ATTN] = w_ref[:, O_DIFF:O_KPE].astype(BF16)
    ws_ref[...] = jnp.zeros_like(ws_ref)
    ws_ref[:, 0:MLA_ROPE_DIM] = w_ref[:, O_KPE:O_GATE].astype(BF16)
    ws_ref[:, MLA_ROPE_DIM:MLA_ROPE_DIM + FOX_HEADS] = w_ref[:, O_FOX_F:O_DIFF].astype(BF16)
    wg_ref[...] = w_ref[:, O_GATE:IN_WIDTH].astype(BF16)


def _repack(w_in):
    k = w_in.shape[1]
    blk = lambda n: pl.BlockSpec((None, REPACK_TK, n), lambda l, i: (l, i, 0))
    return pl.pallas_call(
        _repack_kernel,
        grid=(DEPTH, k // REPACK_TK),
        in_specs=[blk(IN_WIDTH)],
        out_specs=[blk(W_ATTN), blk(LANES), blk(3 * D_MODEL)],
        out_shape=[jax.ShapeDtypeStruct((DEPTH, k, W_ATTN), BF16), jax.ShapeDtypeStruct((DEPTH, k, LANES), BF16),
                   jax.ShapeDtypeStruct((DEPTH, k, 3 * D_MODEL), BF16)],
        compiler_params=_cparams(2),
        name="repack",
    )(w_in)


def _rms(xf, g):
    ms = jnp.mean(jnp.square(xf), axis=-1, keepdims=True)
    return xf * lax.rsqrt(ms + NORM_EPS) * g


def _rope(x_f32, x_b16, r_ref, c, s):
    return x_f32 * c + _dot(x_b16, r_ref[...]) * s


def _store_vt(vt_ref, v, heads):
    vt = v.T.astype(BF16)
    ones = jnp.ones((BF16_ROWS, v.shape[0]), BF16)
    for h in range(heads):
        vt_ref[h * VT_ROWS:h * VT_ROWS + LANES, :] = vt[h * LANES:(h + 1) * LANES]
        vt_ref[h * VT_ROWS + LANES:(h + 1) * VT_ROWS, :] = ones


def _prep_kernel(dq_ref, dk_ref, dv_ref, cq_ref, ckv_ref, ps_ref, cd_ref, sd_ref, cmq_ref, smq_ref, ckp_ref, skp_ref,
                 gq_ref, gkv_ref, wuq_ref, wukv_ref, rd_ref, rmq_ref, rkp_ref, bf_ref,
                 dqo_ref, dko_ref, dvt_ref, mq_ref, mk_ref, mvt_ref, lft_ref):
    cd = cd_ref[...]
    sd = sd_ref[...]
    dq = dq_ref[...]
    dk = dk_ref[...]
    dqo_ref[...] = (_rope(dq.astype(F32), dq, rd_ref, cd, sd) * (DIFF_HEAD_DIM ** -0.5 * LOG2E)).astype(BF16)
    dko_ref[...] = _rope(dk.astype(F32), dk, rd_ref, cd, sd).astype(BF16)
    _store_vt(dvt_ref, dv_ref[...].astype(F32), DIFF_HEADS)

    cqn = _rms(cq_ref[...].astype(F32), gq_ref[...]).astype(BF16)
    q = _dot(cqn, wuq_ref[...])
    mq_ref[...] = _rope(q, q.astype(BF16), rmq_ref, cmq_ref[...], smq_ref[...]).astype(BF16)

    ckvn = _rms(ckv_ref[...].astype(F32), gkv_ref[...]).astype(BF16)
    kv = _dot(ckvn, wukv_ref[...])
    ps = ps_ref[...]
    kpe = _rope(ps, ps.astype(BF16), rkp_ref, ckp_ref[...], skp_ref[...]).astype(BF16)
    for h in range(MLA_HEADS):
        mk_ref[:, h * MLA_QK_PAD:h * MLA_QK_PAD + LANES] = kv[:, h * LANES:(h + 1) * LANES].astype(BF16)
        mk_ref[:, h * MLA_QK_PAD + LANES:(h + 1) * MLA_QK_PAD] = kpe
    _store_vt(mvt_ref, kv[:, MLA_HEADS * LANES:], MLA_HEADS)

    z = ps + bf_ref[...]
    lf = jnp.minimum(z, 0.0) - jnp.log1p(jnp.exp(-jnp.abs(z)))
    lft_ref[...] = lf.T[MLA_ROPE_DIM:MLA_ROPE_DIM + 8, :]


def _prep(pa, ps, tabs, gq, gkv, wuq, wukv, rmats, bf):
    s = pa.shape[0]
    tm = PREP_TM
    row = lambda w: pl.BlockSpec((tm, w), lambda i: (i, 0))
    col = lambda w, c: pl.BlockSpec((tm, w), lambda i: (i, c))
    tr = lambda r: pl.BlockSpec((r, tm), lambda i: (0, i))
    full = lambda a: pl.BlockSpec(a.shape, lambda i: (0,) * a.ndim)
    cd, sd, cmq, smq, ckp, skp = tabs
    rd, rmq, rkp = rmats
    return pl.pallas_call(
        _prep_kernel,
        grid=(s // tm,),
        in_specs=[col(512, 3), col(512, 4), col(512, 5), col(512, 6), col(512, 7), row(LANES),
                  row(512), row(512), row(1024), row(1024), row(LANES), row(LANES),
                  full(gq), full(gkv), full(wuq), full(wukv), full(rd), full(rmq), full(rkp), full(bf)],
        out_specs=[row(512), row(512), tr(4 * VT_ROWS), row(1024), row(1024), tr(4 * VT_ROWS), tr(8)],
        out_shape=[jax.ShapeDtypeStruct((s, 512), BF16), jax.ShapeDtypeStruct((s, 512), BF16),
                   jax.ShapeDtypeStruct((4 * VT_ROWS, s), BF16),
                   jax.ShapeDtypeStruct((s, 1024), BF16), jax.ShapeDtypeStruct((s, 1024), BF16),
                   jax.ShapeDtypeStruct((4 * VT_ROWS, s), BF16), jax.ShapeDtypeStruct((8, s), F32)],
        compiler_params=_cparams(1),
        name="prep",
    )(pa, pa, pa, pa, pa, ps, cd, sd, cmq, smq, ckp, skp, gq, gkv, wuq, wukv, rd, rmq, rkp, bf)


def _cumsum_kernel(x_ref, u_ref, o_ref, carry_ref):
    @pl.when(pl.program_id(0) == 0)
    def _():
        carry_ref[...] = jnp.zeros_like(carry_ref)

    a, b, c = _split3(x_ref[...])
    u = u_ref[...]
    cs = _dot(a, u) + _dot(b, u) + _dot(c, u) + carry_ref[:, :1]
    o_ref[...] = cs
    carry_ref[...] = jnp.broadcast_to(cs[:, CUM_T - 1:CUM_T], carry_ref.shape)


def _cumsum(x, u):
    s = x.shape[1]
    return pl.pallas_call(
        _cumsum_kernel,
        grid=(s // CUM_T,),
        in_specs=[pl.BlockSpec((8, CUM_T), lambda i: (0, i)), pl.BlockSpec(u.shape, lambda i: (0, 0))],
        out_specs=pl.BlockSpec((8, CUM_T), lambda i: (0, i)),
        out_shape=jax.ShapeDtypeStruct((8, s), F32),
        scratch_shapes=[pltpu.VMEM((8, LANES), F32)],
        compiler_params=_cparams(1),
        name="cumsum",
    )(x, u)


N_BIAS = 3


def _foxprep_kernel(q_ref, k_ref, v_ref, cum_ref, qo_ref, ko_ref, vt_ref):
    cum = cum_ref[...]
    beta = (cum[:, 0:1] - cum) * LOG2E
    pieces = [p.astype(F32).T for p in _split3(beta)]
    lane = lax.broadcasted_iota(jnp.int32, (ATT_T, LANES), 1)
    ones = jnp.where(lane < N_BIAS, 1.0, 0.0).astype(BF16)
    q = q_ref[...].astype(F32) * (FOX_HEAD_DIM ** -0.5 * LOG2E)
    for h in range(FOX_HEADS):
        e = jnp.zeros((ATT_T, LANES), F32)
        for n, p in enumerate(pieces):
            e = jnp.where(lane == n, p[:, h:h + 1], e)
        ko_ref[:, h * FOX_QK_PAD:h * FOX_QK_PAD + LANES] = k_ref[:, h * LANES:(h + 1) * LANES]
        ko_ref[:, h * FOX_QK_PAD + LANES:(h + 1) * FOX_QK_PAD] = e.astype(BF16)
        qo_ref[:, h * FOX_QK_PAD:h * FOX_QK_PAD + LANES] = q[:, h * LANES:(h + 1) * LANES].astype(BF16)
        qo_ref[:, h * FOX_QK_PAD + LANES:(h + 1) * FOX_QK_PAD] = ones
    _store_vt(vt_ref, v_ref[...].astype(F32), FOX_HEADS)


def _foxprep(pa, cum):
    s = pa.shape[0]
    t = ATT_T
    w = FOX_HEADS * FOX_QK_PAD
    return pl.pallas_call(
        _foxprep_kernel,
        grid=(s // t,),
        in_specs=[pl.BlockSpec((t, 512), lambda i: (i, 0)), pl.BlockSpec((t, 512), lambda i: (i, 1)),
                  pl.BlockSpec((t, 512), lambda i: (i, 2)), pl.BlockSpec((8, t), lambda i: (0, i))],
        out_specs=[pl.BlockSpec((t, w), lambda i: (i, 0)), pl.BlockSpec((t, w), lambda i: (i, 0)),
                   pl.BlockSpec((FOX_HEADS * VT_ROWS, t), lambda i: (0, i))],
        out_shape=[jax.ShapeDtypeStruct((s, w), BF16), jax.ShapeDtypeStruct((s, w), BF16),
                   jax.ShapeDtypeStruct((FOX_HEADS * VT_ROWS, s), BF16)],
        compiler_params=_cparams(1),
        name="foxprep",
    )(pa, pa, pa, cum)


def _flash_kernel(*refs, mode, lam_init):
    t = ATT_T
    if mode == "fox":
        q_ref, k_ref, vt_ref, ck_ref, o_ref, sa_ref, sb_ref, xa_ref, xb_ref, m_ref, acc_ref = refs
    elif mode == "diff":
        q_ref, k_ref, vt_ref, lam_ref, gd_ref, o_ref, sa_ref, sb_ref, xa_ref, xb_ref, m_ref, acc_ref = refs
    else:
        q_ref, k_ref, vt_ref, o_ref, sa_ref, sb_ref, xa_ref, xb_ref, m_ref, acc_ref = refs
    i = pl.program_id(1)
    off_i = pl.multiple_of(i * ATT_TQ, ATT_TQ)
    m_ref[...] = jnp.full(m_ref.shape, NEG, F32)
    acc_ref[...] = jnp.zeros_like(acc_ref)

    q = q_ref[...]
    if mode == "diff":
        lane = lax.broadcasted_iota(jnp.int32, q.shape, 1)
        zero = jnp.zeros_like(q)
        qs = [jnp.where(lane < DIFF_HEAD_DIM, q, zero), jnp.where(lane >= DIFF_HEAD_DIM, q, zero)]
    else:
        qs = [q]

    buf_a = (sa_ref, xa_ref)
    buf_b = (sb_ref, xb_ref)

    def scores(off, buf, q0=0):
        s_ref, x_ref = buf
        w = ATT_TQ - q0
        kb = k_ref[pl.ds(off, t), :]
        for n, qn in enumerate(qs):
            st = _dot_nt(kb, qn[q0:])
            s_ref[n, :, :w] = st
            x_ref[n, :, :w] = jnp.max(st, axis=0, keepdims=True)

    def consume(off, buf, masked, q0=0):
        s_ref, x_ref = buf
        w = ATT_TQ - q0
        vtb = vt_ref[:, pl.ds(off, t)]
        for n in range(len(qs)):
            st = s_ref[n, :, :w]
            if masked:
                key_id = lax.broadcasted_iota(jnp.int32, st.shape, 0) + (off - off_i)
                qry_id = lax.broadcasted_iota(jnp.int32, st.shape, 1) + q0
                st = jnp.where(key_id <= qry_id, st, NEG)
                m = jnp.max(st, axis=0, keepdims=True)
            else:
                m = x_ref[n, :, :w]
            p = jnp.exp2((st - m).astype(BF16))
            o = _dot(vtb, p)
            if mode == "fox":
                m = m + (ck_ref[:, pl.ds(off_i, LANES)][:, :1] - ck_ref[:, pl.ds(off, LANES)][:, :1]) * LOG2E
            m_prev = m_ref[n, :, q0:]
            m_new = jnp.maximum(m_prev, m)
            acc_ref[n, :, q0:] = jnp.exp2(m_prev - m_new) * acc_ref[n, :, q0:] + jnp.exp2(m - m_new) * o
            m_ref[n, :, q0:] = m_new

    scores(0, buf_a)

    def pair(jj, carry):
        off0 = pl.multiple_of(jj * (2 * t), 2 * t)
        off1 = pl.multiple_of(off0 + t, t)
        scores(off1, buf_b)
        consume(off0, buf_a, False)
        scores(pl.multiple_of(off1 + t, t), buf_a)
        consume(off1, buf_b, False)
        return carry

    lax.fori_loop(0, i, pair, 0)
    off_d = pl.multiple_of(off_i + t, t)
    scores(off_d, buf_b, q0=t)
    consume(off_i, buf_a, True)
    consume(off_d, buf_b, True, q0=t)

    def normalised(n):
        acc = acc_ref[n]
        return acc[:LANES] / acc[LANES:LANES + 1]

    if mode == "diff":
        dl = lam_ref[...]
        lam = (jnp.exp(jnp.sum(dl[0:1] * dl[1:2], axis=1, keepdims=True))
               - jnp.exp(jnp.sum(dl[2:3] * dl[3:4], axis=1, keepdims=True)) + lam_init)
        o = _rms((normalised(0) - lam * normalised(1)).T, gd_ref[...]) * (1.0 - lam_init)
    else:
        o = normalised(0).T
    o_ref[...] = o.astype(o_ref.dtype)


def _flash(mode, q_arr, k_arr, vt_arr, dk, heads, extra=(), lam_init=0.0):
    s = q_arr.shape[0]
    t = ATT_T
    tq = ATT_TQ
    dv = LANES
    n_stream = 2 if mode == "diff" else 1
    in_specs = [pl.BlockSpec((tq, dk), lambda h, i: (i, h)),
                pl.BlockSpec((s, dk), lambda h, i: (0, h)),
                pl.BlockSpec((VT_ROWS, s), lambda h, i: (h, 0))]
    if mode == "fox":
        in_specs.append(pl.BlockSpec((None, 1, s), lambda h, i: (h, 0, 0)))
    elif mode == "diff":
        in_specs += [pl.BlockSpec(extra[0].shape, lambda h, i: (0, 0)),
                     pl.BlockSpec(extra[1].shape, lambda h, i: (0, 0))]
    return pl.pallas_call(
        functools.partial(_flash_kernel, mode=mode, lam_init=lam_init),
        grid=(heads, s // tq),
        in_specs=in_specs,
        out_specs=pl.BlockSpec((tq, dv), lambda h, i: (i, h)),
        out_shape=jax.ShapeDtypeStruct((s, heads * dv), BF16),
        scratch_shapes=[pltpu.VMEM((n_stream, t, tq), F32), pltpu.VMEM((n_stream, t, tq), F32),
                        pltpu.VMEM((n_stream, 1, tq), F32), pltpu.VMEM((n_stream, 1, tq), F32),
                        pltpu.VMEM((n_stream, 1, tq), F32), pltpu.VMEM((n_stream, VT_ROWS, tq), F32)],
        compiler_params=_cparams(2),
        name="flash_" + mode,
    )(q_arr, k_arr, vt_arr, *extra)


def _layer_norm(z, g, b):
    mu = jnp.mean(z, axis=-1, keepdims=True)
    zc = z - mu
    var = jnp.mean(jnp.square(zc), axis=-1, keepdims=True)
    return zc * lax.rsqrt(var + NORM_EPS) * g + b


def _first_index_of_max(vals):
    best = vals[0]
    for v in vals[1:]:
        best = jnp.maximum(best, v)
    idx = jnp.full(best.shape, len(vals) - 1, jnp.int32)
    for j in range(len(vals) - 2, -1, -1):
        idx = jnp.where(vals[j] == best, j, idx)
    return best, idx


def _post_kernel(of_ref, od_ref, om_ref, g0_ref, g1_ref, g2_ref, x_ref, wf_ref, wd_ref, wm_ref, wo_ref,
                 lg_ref, lb_ref, wr_ref, rb_ref, x1_ref, h16_ref, eid_ref, wt_ref):
    tm = POST_TM
    merged = (g0_ref[...].astype(F32) * _dot(of_ref[...], wf_ref[...])
              + g1_ref[...].astype(F32) * _dot(od_ref[...], wd_ref[...])
              + g2_ref[...].astype(F32) * _dot(om_ref[...], wm_ref[...]))
    mix = _dot(merged.astype(BF16), wo_ref[...])
    x1 = _layer_norm(DEEPNORM_ALPHA * x_ref[...] + mix, lg_ref[...], lb_ref[...])
    x1_ref[...] = x1
    for c in range(ROW_CHUNKS):
        h16_ref[pl.ds(c, tm, stride=ROW_CHUNKS), :] = x1[:, c * LANES:(c + 1) * LANES]

    e = N_EXPERTS
    prod = [_dot(p, wr_ref[...]).T for p in _split3(x1)]
    logits = (prod[0][0:e] + (prod[0][e:2 * e] + prod[1][0:e])
              + (prod[0][2 * e:3 * e] + prod[1][e:2 * e] + prod[2][0:e]))
    scores = jax.nn.sigmoid(logits)
    biased = scores + rb_ref[...]
    a = [biased[j * 8:(j + 1) * 8] for j in range(EXPERTS_PER_GROUP)]
    u = [scores[j * 8:(j + 1) * 8] for j in range(EXPERTS_PER_GROUP)]
    gs = a[0] + a[1]
    for j0 in range(EXPERTS_PER_GROUP):
        for j1 in range(j0 + 1, EXPERTS_PER_GROUP):
            gs = jnp.maximum(gs, a[j0] + a[j1])
    gmax = jnp.max(gs, axis=0, keepdims=True)
    gid = lax.broadcasted_iota(jnp.int32, gs.shape, 0)
    best = jnp.min(jnp.where(gs == gmax, gid, N_GROUPS), axis=0, keepdims=True)
    sel = gid == best
    ing = [jnp.sum(jnp.where(sel, a[j], 0.0), axis=0, keepdims=True) for j in range(EXPERTS_PER_GROUP)]
    unb = [jnp.sum(jnp.where(sel, u[j], 0.0), axis=0, keepdims=True) for j in range(EXPERTS_PER_GROUP)]
    _, l1 = _first_index_of_max(ing)
    ing2 = [jnp.where(l1 == j, -jnp.inf, ing[j]) for j in range(EXPERTS_PER_GROUP)]
    _, l2 = _first_index_of_max(ing2)
    w1 = jnp.zeros_like(unb[0])
    w2 = jnp.zeros_like(unb[0])
    for j in range(EXPERTS_PER_GROUP):
        w1 = jnp.where(l1 == j, unb[j], w1)
        w2 = jnp.where(l2 == j, unb[j], w2)
    tot = w1 + w2
    eid_ref[...] = jnp.zeros_like(eid_ref)
    wt_ref[...] = jnp.zeros_like(wt_ref)
    eid_ref[0:1, :] = best * EXPERTS_PER_GROUP + l1
    eid_ref[1:2, :] = best * EXPERTS_PER_GROUP + l2
    wt_ref[0:1, :] = w1 / tot
    wt_ref[1:2, :] = w2 / tot


def _post(o_fox, o_diff, o_mla, gates, x, wf, wd, wm, wo, lg, lb, wr, rb):
    s = x.shape[0]
    tm = POST_TM
    row = lambda w: pl.BlockSpec((tm, w), lambda i: (i, 0))
    full = lambda a: pl.BlockSpec(a.shape, lambda i: (0,) * a.ndim, pipeline_mode=pl.Buffered(1))
    return pl.pallas_call(
        _post_kernel,
        grid=(s // tm,),
        in_specs=[row(512), row(512), row(512),
                  pl.BlockSpec((tm, D_MODEL), lambda i: (i, 0)),
                  pl.BlockSpec((tm, D_MODEL), lambda i: (i, 1)),
                  pl.BlockSpec((tm, D_MODEL), lambda i: (i, 2)),
                  row(D_MODEL), full(wf), full(wd), full(wm), full(wo), full(lg), full(lb), full(wr), full(rb)],
        out_specs=[row(D_MODEL), pl.BlockSpec((tm * ROW_CHUNKS, LANES), lambda i: (i, 0)),
                   pl.BlockSpec((8, tm), lambda i: (0, i)), pl.BlockSpec((8, tm), lambda i: (0, i))],
        out_shape=[jax.ShapeDtypeStruct((s, D_MODEL), F32),
                   jax.ShapeDtypeStruct((s * ROW_CHUNKS, LANES), F32),
                   jax.ShapeDtypeStruct((8, s), jnp.int32), jax.ShapeDtypeStruct((8, s), F32)],
        compiler_params=_cparams(1),
        name="post",
    )(o_fox, o_diff, o_mla, gates, gates, gates, x, wf, wd, wm, wo, lg, lb, wr, rb)


def _rank_kernel(eid_ref, us_ref, rank_ref, cnt_ref, carry_ref):
    @pl.when(pl.program_id(0) == 0)
    def _():
        carry_ref[...] = jnp.zeros_like(carry_ref)

    e0 = eid_ref[0:1, :]
    e1 = eid_ref[1:2, :]
    eio = lax.broadcasted_iota(jnp.int32, (N_EXPERTS, RANK_T), 0)
    hit0 = eio == e0
    hit1 = eio == e1
    onehot = jnp.where(hit0 | hit1, 1.0, 0.0)
    before = _dot(onehot.astype(BF16), us_ref[...]) + carry_ref[:, :1]
    rank_ref[...] = jnp.zeros_like(rank_ref)
    rank_ref[0:1, :] = jnp.sum(jnp.where(hit0, before, 0.0), axis=0, keepdims=True).astype(jnp.int32)
    rank_ref[1:2, :] = jnp.sum(jnp.where(hit1, before, 0.0), axis=0, keepdims=True).astype(jnp.int32)
    carry_ref[...] = carry_ref[...] + jnp.sum(onehot, axis=1, keepdims=True)
    cnt_ref[...] = carry_ref[...]


def _rank(eid, us):
    s = eid.shape[1]
    return pl.pallas_call(
        _rank_kernel,
        grid=(s // RANK_T,),
        in_specs=[pl.BlockSpec((8, RANK_T), lambda i: (0, i)), pl.BlockSpec(us.shape, lambda i: (0, 0))],
        out_specs=[pl.BlockSpec((8, RANK_T), lambda i: (0, i)),
                   pl.BlockSpec((N_EXPERTS, LANES), lambda i: (0, 0))],
        out_shape=[jax.ShapeDtypeStruct((8, s), jnp.int32), jax.ShapeDtypeStruct((N_EXPERTS, LANES), F32)],
        scratch_shapes=[pltpu.VMEM((N_EXPERTS, LANES), F32)],
        compiler_params=_cparams(1),
        name="rank",
    )(eid, us)


def _dest_kernel(cnt_ref, eid_ref, rank_ref, lt_ref, dest_ref, blk_ref):
    nb = jnp.floor((cnt_ref[...] + (MOE_R - 1)) * (1.0 / MOE_R))
    incl = _dot(lt_ref[...], nb.astype(BF16))
    start = (incl - nb)[:, :1] * MOE_R
    eio = lax.broadcasted_iota(jnp.int32, (N_EXPERTS, DEST_T), 0)
    dest_ref[...] = jnp.zeros_like(dest_ref)
    for k in range(TOP_K):
        base = jnp.sum(jnp.where(eio == eid_ref[k:k + 1, :], start, 0.0), axis=0, keepdims=True)
        dest_ref[k:k + 1, :] = base.astype(jnp.int32) + rank_ref[k:k + 1, :]
    bid = lax.broadcasted_iota(jnp.int32, (N_EXPERTS, BLK_LANES), 1).astype(F32)
    ended = jnp.sum(jnp.where(incl[:, :1] <= bid, 1.0, 0.0), axis=0, keepdims=True)
    blk_ref[...] = jnp.zeros_like(blk_ref)
    blk_ref[0:1, :] = jnp.minimum(ended, N_EXPERTS - 1.0).astype(jnp.int32)
    blk_ref[1:2, :] = jnp.broadcast_to(incl[N_EXPERTS - 1:N_EXPERTS, :1], (1, BLK_LANES)).astype(jnp.int32)


def _dest(cnt, eid, rank, lt):
    s = eid.shape[1]
    return pl.pallas_call(
        _dest_kernel,
        grid=(s // DEST_T,),
        in_specs=[pl.BlockSpec(cnt.shape, lambda i: (0, 0)),
                  pl.BlockSpec((8, DEST_T), lambda i: (0, i)), pl.BlockSpec((8, DEST_T), lambda i: (0, i)),
                  pl.BlockSpec(lt.shape, lambda i: (0, 0))],
        out_specs=[pl.BlockSpec((8, DEST_T), lambda i: (0, i)), pl.BlockSpec((8, BLK_LANES), lambda i: (0, 0))],
        out_shape=[jax.ShapeDtypeStruct((8, s), jnp.int32), jax.ShapeDtypeStruct((8, BLK_LANES), jnp.int32)],
        compiler_params=_cparams(1),
        name="dest",
    )(cnt, eid, rank, lt)


def _row_copy(src_ref, src_row, dst_ref, dst_row, sem):
    def rows(row):
        start = row * ROW_CHUNKS
        return pl.ds(start if isinstance(row, int) else pl.multiple_of(start, ROW_CHUNKS), ROW_CHUNKS)

    return pltpu.make_async_copy(src_ref.at[rows(src_row)], dst_ref.at[rows(dst_row)], sem)


def _dispatch_kernel(dest_ref, h_ref, xs_in_ref, xs_ref, sem):
    del xs_in_ref
    base = pl.program_id(0) * DISP_T

    def issue(r, carry):
        for k in range(TOP_K):
            _row_copy(h_ref, r, xs_ref, dest_ref[k * SEQ + base + r], sem).start(priority=k)
        return carry

    lax.fori_loop(0, DISP_T, issue, 0, unroll=ISSUE_UNROLL)
    for k in range(TOP_K):
        pltpu.make_async_copy(h_ref, xs_ref.at[pl.ds(0, DISP_T * ROW_CHUNKS)], sem).wait()


def _dispatch(dest_flat, h16, xs_zero):
    return pl.pallas_call(
        _dispatch_kernel,
        grid_spec=pltpu.PrefetchScalarGridSpec(
            num_scalar_prefetch=1, grid=(SEQ // DISP_T,),
            in_specs=[pl.BlockSpec((DISP_T * ROW_CHUNKS, LANES), lambda i, d: (i, 0)),
                      pl.BlockSpec(memory_space=pl.ANY)],
            out_specs=pl.BlockSpec(memory_space=pl.ANY),
            scratch_shapes=[pltpu.SemaphoreType.DMA(())]),
        out_shape=jax.ShapeDtypeStruct(xs_zero.shape, xs_zero.dtype),
        input_output_aliases={2: 0},
        compiler_params=_cparams(1),
        name="dispatch",
    )(dest_flat, h16, xs_zero)


def _moe_kernel(be_ref, nu_ref, xs_ref, wgu_ref, wout_ref, y_ref, lhs_ref):
    del be_ref
    used = pl.program_id(0) < nu_ref[0]

    @pl.when(jnp.logical_not(used))
    def _():
        y_ref[...] = jnp.zeros_like(y_ref)

    @pl.when(used)
    def _():
        for c in range(ROW_CHUNKS):
            lhs_ref[:, c * LANES:(c + 1) * LANES] = xs_ref[pl.ds(c, MOE_R, stride=ROW_CHUNKS), :].astype(BF16)
        gu = _dot(lhs_ref[...], wgu_ref[...])
        hmid = (jax.nn.silu(gu[:, :D_FF_EXPERT]) * gu[:, D_FF_EXPERT:]).astype(BF16)
        y = _dot(hmid, wout_ref[...])
        for c in range(ROW_CHUNKS):
            y_ref[pl.ds(c, MOE_R, stride=ROW_CHUNKS), :] = y[:, c * LANES:(c + 1) * LANES]


def _moe(blk_e, n_used, xs, wgu, wout, layer):
    last = lambda b, nu: jnp.minimum(b, nu[0] - 1)
    rows = pl.BlockSpec((MOE_R * ROW_CHUNKS, LANES), lambda b, be, nu: (last(b, nu), 0))
    return pl.pallas_call(
        _moe_kernel,
        grid_spec=pltpu.PrefetchScalarGridSpec(
            num_scalar_prefetch=2, grid=(MOE_NB,),
            in_specs=[rows,
                      pl.BlockSpec((None, None) + wgu.shape[2:], lambda b, be, nu: (layer, be[last(b, nu)], 0, 0)),
                      pl.BlockSpec((None, None) + wout.shape[2:], lambda b, be, nu: (layer, be[last(b, nu)], 0, 0))],
            out_specs=pl.BlockSpec((MOE_R * ROW_CHUNKS, LANES), lambda b, be, nu: (b, 0)),
            scratch_shapes=[pltpu.VMEM((MOE_R, D_MODEL), BF16)]),
        out_shape=jax.ShapeDtypeStruct(xs.shape, F32),
        compiler_params=_cparams(1),
        name="moe",
    )(blk_e, n_used, xs, wgu, wout)


def _combine_kernel(dest_ref, y_ref, x1_ref, wt_ref, g_ref, b_ref, o_ref, buf_ref, sem):
    i = pl.program_id(0)
    slot = i & 1

    def issue(tile, sl):
        def body(r, carry):
            for k in range(TOP_K):
                _row_copy(y_ref, dest_ref[k * SEQ + tile * COMB_T + r], buf_ref.at[sl, k], r,
                          sem.at[sl]).start(priority=k)
            return carry

        lax.fori_loop(0, COMB_T, body, 0, unroll=ISSUE_UNROLL)

    @pl.when(i == 0)
    def _():
        issue(0, 0)

    @pl.when(i + 1 < pl.num_programs(0))
    def _():
        issue(i + 1, 1 - slot)

    for k in range(TOP_K):
        pltpu.make_async_copy(y_ref.at[pl.ds(0, COMB_T * ROW_CHUNKS)], buf_ref.at[slot, k], sem.at[slot]).wait()

    wcol = wt_ref[...].T
    w0 = wcol[:, 0:1]
    w1 = wcol[:, 1:2]
    ffn = jnp.concatenate(
        [w0 * buf_ref[slot, 0, pl.ds(c, COMB_T, stride=ROW_CHUNKS), :]
         + w1 * buf_ref[slot, 1, pl.ds(c, COMB_T, stride=ROW_CHUNKS), :] for c in range(ROW_CHUNKS)], axis=1)
    o_ref[...] = _layer_norm(DEEPNORM_ALPHA * x1_ref[...] + ffn, g_ref[...], b_ref[...])


def _combine(dest_flat, y16, x1, wt, g, b):
    tm = COMB_T
    return pl.pallas_call(
        _combine_kernel,
        grid_spec=pltpu.PrefetchScalarGridSpec(
            num_scalar_prefetch=1, grid=(SEQ // tm,),
            in_specs=[pl.BlockSpec(memory_space=pl.ANY),
                      pl.BlockSpec((tm, D_MODEL), lambda i, d: (i, 0)),
                      pl.BlockSpec((8, tm), lambda i, d: (0, i)),
                      pl.BlockSpec(g.shape, lambda i, d: (0, 0)), pl.BlockSpec(b.shape, lambda i, d: (0, 0))],
            out_specs=pl.BlockSpec((tm, D_MODEL), lambda i, d: (i, 0)),
            scratch_shapes=[pltpu.VMEM((2, TOP_K, tm * ROW_CHUNKS, LANES), F32), pltpu.SemaphoreType.DMA((2,))]),
        out_shape=jax.ShapeDtypeStruct(x1.shape, F32),
        compiler_params=_cparams(1),
        name="combine",
    )(dest_flat, y16, x1, wt, g, b)


def _rot_matrix(width, groups, half):
    r = np.zeros((width, width), np.float32)
    for o in groups:
        for j in range(half):
            r[o + j + half, o + j] = -1.0
            r[o + j, o + half + j] = 1.0
    return jnp.asarray(r, BF16)


def _rope_tables():
    pos = jnp.arange(SEQ, dtype=F32)

    def cs(dim):
        inv_freq = ROPE_THETA ** (-jnp.arange(0, dim, 2, dtype=F32) / dim)
        ang = pos[:, None] * inv_freq[None, :]
        return jnp.cos(ang), jnp.sin(ang)

    cp, sp = cs(DIFF_ROT_DIM)
    one = jnp.ones((SEQ, DIFF_HEAD_DIM - DIFF_ROT_DIM), F32)
    cd = jnp.tile(jnp.concatenate([cp, cp, one], axis=1), (1, 2 * DIFF_HEADS))
    sd = jnp.tile(jnp.concatenate([sp, sp, 0.0 * one], axis=1), (1, 2 * DIFF_HEADS))
    cm, sm = cs(MLA_ROPE_DIM)
    scale = MLA_QK_DIM ** -0.5 * LOG2E
    ones_n = jnp.ones((SEQ, MLA_NOPE_DIM), F32)
    pad = jnp.zeros((SEQ, MLA_QK_PAD - MLA_QK_DIM), F32)
    cmq = jnp.tile(jnp.concatenate([ones_n, cm, cm, pad], axis=1) * scale, (1, MLA_HEADS))
    smq = jnp.tile(jnp.concatenate([0.0 * ones_n, sm, sm, pad], axis=1) * scale, (1, MLA_HEADS))
    ckp = jnp.concatenate([cm, cm, pad], axis=1)
    skp = jnp.concatenate([sm, sm, pad], axis=1)
    return cd, sd, cmq, smq, ckp, skp


def _constants():
    rd = _rot_matrix(512, [c * DIFF_HEAD_DIM for c in range(2 * DIFF_HEADS)], DIFF_ROT_DIM // 2)
    rmq = _rot_matrix(MLA_HEADS * MLA_QK_PAD, [h * MLA_QK_PAD + MLA_NOPE_DIM for h in range(MLA_HEADS)],
                      MLA_ROPE_DIM // 2)
    rkp = _rot_matrix(LANES, [0], MLA_ROPE_DIM // 2)
    tri = np.arange(CUM_T)
    u_incl = jnp.asarray(tri[:, None] <= tri[None, :], BF16)
    tri = np.arange(RANK_T)
    u_strict = jnp.asarray(tri[:, None] < tri[None, :], BF16)
    tri = np.arange(N_EXPERTS)
    l_incl = jnp.asarray(tri[:, None] >= tri[None, :], BF16)
    return (rd, rmq, rkp), u_incl, u_strict, l_incl


def kernel(x, w_in, b_fox_f, b_gate, diff_lambda, g_diff, g_mla_q, g_mla_kv, w_mla_uq, w_mla_ukv, w_fox_up, w_diff_up, w_mla_up, w_o, ln1_g, ln1_b, ln2_g, ln2_b, w_router, router_bias, w_exp_in, w_exp_out):
    s = x.shape[1]
    assert x.shape == (1, SEQ, D_MODEL) and w_in.shape == (DEPTH, D_MODEL, IN_WIDTH)
    xf = x.reshape(s, D_MODEL)
    tabs = _rope_tables()
    rmats, u_incl, u_strict, l_incl = _constants()
    w_a, w_s, w_g = _repack(w_in)

    perm = np.array([g * EXPERTS_PER_GROUP + j for j in range(EXPERTS_PER_GROUP) for g in range(N_GROUPS)])
    wr = jnp.concatenate(list(_split3(w_router[:, perm]))
                         + [jnp.zeros((D_MODEL, LANES - 3 * N_EXPERTS), BF16)], axis=1)
    rb = router_bias[perm].reshape(N_EXPERTS, 1)
    wgu = w_exp_in.astype(BF16)
    wout = w_exp_out.astype(BF16)
    zero_a = jnp.zeros((1, W_ATTN), F32)
    zero_s = jnp.zeros((1, LANES), F32)

    for i in range(DEPTH):
        pa = _linear(xf, w_a, i, zero_a, None, BF16, LIN_TN, "inproj_attn")
        ps = _linear(xf, w_s, i, zero_s, None, F32, LANES, "inproj_small")
        gates = _linear(xf, w_g, i, b_gate[i].reshape(1, -1), "sigmoid", BF16, LIN_TN, "inproj_gate")

        uq = w_mla_uq[i].reshape(MLA_Q_RANK, MLA_HEADS, MLA_QK_DIM)
        wuq = jnp.concatenate([uq, jnp.zeros((MLA_Q_RANK, MLA_HEADS, MLA_QK_PAD - MLA_QK_DIM), F32)], axis=2)
        wuq = wuq.reshape(MLA_Q_RANK, MLA_HEADS * MLA_QK_PAD).astype(BF16)
        ukv = w_mla_ukv[i].reshape(MLA_KV_RANK, MLA_HEADS, MLA_NOPE_DIM + MLA_V_DIM)
        wukv = jnp.concatenate([ukv[:, :, :MLA_NOPE_DIM].reshape(MLA_KV_RANK, -1),
                                ukv[:, :, MLA_NOPE_DIM:].reshape(MLA_KV_RANK, -1)], axis=1).astype(BF16)
        bf = jnp.zeros((1, LANES), F32).at[0, MLA_ROPE_DIM:MLA_ROPE_DIM + FOX_HEADS].set(b_fox_f[i])

        dq, dk, dvt, mq, mk, mvt, lft = _prep(pa, ps, tabs, g_mla_q[i].reshape(1, -1), g_mla_kv[i].reshape(1, -1),
                                              wuq, wukv, rmats, bf)
        cum = _cumsum(lft, u_incl)
        fq, fk, fvt = _foxprep(pa, cum)
        cum_row = cum[:FOX_HEADS].reshape(FOX_HEADS, 1, s)

        lam_init = 0.8 - 0.6 * math.exp(-0.3 * i)
        o_fox = _flash("fox", fq, fk, fvt, FOX_QK_PAD, FOX_HEADS, extra=(cum_row,))
        o_diff = _flash("diff", dq, dk, dvt, 2 * DIFF_HEAD_DIM, DIFF_HEADS,
                        extra=(diff_lambda[i], g_diff[i].reshape(1, -1)), lam_init=lam_init)
        o_mla = _flash("mla", mq, mk, mvt, MLA_QK_PAD, MLA_HEADS)

        x1, h16, eid, wt = _post(o_fox, o_diff, o_mla, gates, xf,
                                 w_fox_up[i].astype(BF16), w_diff_up[i].astype(BF16), w_mla_up[i].astype(BF16),
                                 w_o[i].astype(BF16), ln1_g[i].reshape(1, -1), ln1_b[i].reshape(1, -1), wr, rb)

        rank, cnt = _rank(eid, u_strict)
        dest, blk = _dest(cnt, eid, rank, l_incl)
        dest_flat = dest[:TOP_K].reshape(TOP_K * s)
        xs = _dispatch(dest_flat, h16, jnp.zeros((MOE_ROWS * ROW_CHUNKS, LANES), F32))
        y16 = _moe(blk[0], blk[1, :1], xs, wgu, wout, i)
        xf = _combine(dest_flat, y16, x1, wt, ln2_g[i].reshape(1, -1), ln2_b[i].reshape(1, -1))
    return xf.reshape(1, s, D_MODEL)
```

```python
import functools
import math

import numpy as np
import jax
import jax.numpy as jnp
from jax import lax
from jax.experimental import pallas as pl
from jax.experimental.pallas import tpu as pltpu

F32 = jnp.float32
BF16 = jnp.bfloat16

D_MODEL = 2048
SEQ = 16384
DEPTH = 2
ROPE_THETA = 500000.0
NORM_EPS = 1e-5
FOX_HEADS = 4
FOX_HEAD_DIM = 128
DIFF_HEADS = 4
DIFF_HEAD_DIM = 64
DIFF_V_DIM = 128
DIFF_ROT_DIM = 16
MLA_HEADS = 4
MLA_Q_RANK = 512
MLA_KV_RANK = 512
MLA_NOPE_DIM = 128
MLA_ROPE_DIM = 64
MLA_V_DIM = 128
MLA_QK_DIM = MLA_NOPE_DIM + MLA_ROPE_DIM
N_EXPERTS = 32
N_GROUPS = 8
EXPERTS_PER_GROUP = 4
TOP_K = 2
D_FF_EXPERT = 1408
DEEPNORM_ALPHA = (2 * DEPTH) ** 0.25

LANES = 128
ROW_CHUNKS = D_MODEL // LANES
MLA_QK_PAD = 256
FOX_QK_PAD = 256
BF16_ROWS = 16
VT_ROWS = LANES + BF16_ROWS
LOG2E = math.log2(math.e)
VMEM_LIMIT = 56 * 1024 * 1024

LIN_TM = 1024
LIN_TN = 1024
PREP_TM = 512
CUM_T = 512
ATT_T = 512
ATT_TQ = 2 * ATT_T
POST_TM = 256
RANK_T = 512
DEST_T = 2048
DISP_T = 512
MOE_R = 256
MOE_NB = (SEQ * TOP_K + N_EXPERTS * (MOE_R - 1) + MOE_R - 1) // MOE_R
MOE_ROWS = MOE_NB * MOE_R
BLK_LANES = 256
COMB_T = 256
ISSUE_UNROLL = 8
NEG = -1e30


def _cparams(n_axes, **kw):
    return pltpu.CompilerParams(dimension_semantics=("arbitrary",) * n_axes,
                                vmem_limit_bytes=VMEM_LIMIT, **kw)


def _dot(a, b):
    return jnp.dot(a, b, preferred_element_type=F32)


def _dot_nt(a, b):
    return lax.dot_general(a, b, (((1,), (1,)), ((), ())), preferred_element_type=F32)


def _split3(x):
    a = x.astype(BF16)
    r = x - a.astype(F32)
    b = r.astype(BF16)
    c = (r - b.astype(F32)).astype(BF16)
    return a, b, c


def _linear_kernel(x_ref, w_ref, b_ref, o_ref, xb_ref, *, act):
    @pl.when(pl.program_id(1) == 0)
    def _():
        xb_ref[...] = x_ref[...].astype(BF16)

    acc = _dot(xb_ref[...], w_ref[...]) + b_ref[...]
    if act == "sigmoid":
        acc = jax.nn.sigmoid(acc)
    o_ref[...] = acc.astype(o_ref.dtype)


def _linear(x, w, layer, b, act, out_dtype, tn, name):
    m, k = x.shape
    n = w.shape[2]
    return pl.pallas_call(
        functools.partial(_linear_kernel, act=act),
        grid=(m // LIN_TM, n // tn),
        in_specs=[pl.BlockSpec((LIN_TM, k), lambda i, j: (i, 0)),
                  pl.BlockSpec((None, k, tn), lambda i, j: (layer, 0, j)),
                  pl.BlockSpec((1, tn), lambda i, j: (0, j))],
        out_specs=pl.BlockSpec((LIN_TM, tn), lambda i, j: (i, j)),
        out_shape=jax.ShapeDtypeStruct((m, n), out_dtype),
        scratch_shapes=[pltpu.VMEM((LIN_TM, k), BF16)],
        compiler_params=_cparams(2),
        name=name,
    )(x, w, b)


O_FOX_F = 3 * 512
O_DIFF = O_FOX_F + FOX_HEADS
O_MLA = O_DIFF + 3 * 512
O_KPE = O_MLA + 2 * 512
O_GATE = O_KPE + MLA_ROPE_DIM
IN_WIDTH = O_GATE + 3 * D_MODEL
W_ATTN = 4096
REPACK_TK = 256


def _repack_kernel(w_ref, wa_ref, ws_ref, wg_ref):
    wa_ref[:, 0:O_FOX_F] = w_ref[:, 0:O_FOX_F].astype(BF16)
    wa_ref[:, O_FOX_F:W_ATTN] = w_ref[:, O_DIFF:O_KPE].astype(BF16)
    ws_ref[...] = jnp.zeros_like(ws_ref)
    ws_ref[:, 0:MLA_ROPE_DIM] = w_ref[:, O_KPE:O_GATE].astype(BF16)
    ws_ref[:, MLA_ROPE_DIM:MLA_ROPE_DIM + FOX_HEADS] = w_ref[:, O_FOX_F:O_DIFF].astype(BF16)
    wg_ref[...] = w_ref[:, O_GATE:IN_WIDTH].astype(BF16)


def _repack(w_in):
    k = w_in.shape[1]
    blk = lambda n: pl.BlockSpec((None, REPACK_TK, n), lambda l, i: (l, i, 0))
    return pl.pallas_call(
        _repack_kernel,
        grid=(DEPTH, k // REPACK_TK),
        in_specs=[blk(IN_WIDTH)],
        out_specs=[blk(W_ATTN), blk(LANES), blk(3 * D_MODEL)],
        out_shape=[jax.ShapeDtypeStruct((DEPTH, k, W_ATTN), BF16), jax.ShapeDtypeStruct((DEPTH, k, LANES), BF16),
                   jax.ShapeDtypeStruct((DEPTH, k, 3 * D_MODEL), BF16)],
        compiler_params=_cparams(2),
        name="repack",
    )(w_in)


def _rms(xf, g):
    ms = jnp.mean(jnp.square(xf), axis=-1, keepdims=True)
    return xf * lax.rsqrt(ms + NORM_EPS) * g


def _rope(x_f32, x_b16, r_ref, c, s):
    return x_f32 * c + _dot(x_b16, r_ref[...]) * s


def _store_vt(vt_ref, v, heads):
    vt = v.T
    ones = jnp.ones((BF16_ROWS, v.shape[0]), BF16)
    for h in range(heads):
        vt_ref[h * VT_ROWS:h * VT_ROWS + LANES, :] = vt[h * LANES:(h + 1) * LANES]
        vt_ref[h * VT_ROWS + LANES:(h + 1) * VT_ROWS, :] = ones


def _prep_kernel(dq_ref, dk_ref, dv_ref, cq_ref, ckv_ref, ps_ref, cd_ref, sd_ref, cmq_ref, smq_ref, ckp_ref, skp_ref,
                 gq_ref, gkv_ref, wuq_ref, wukv_ref, rd_ref, rmq_ref, rkp_ref, bf_ref,
                 dqo_ref, dko_ref, dvt_ref, mq_ref, mk_ref, mvt_ref, lft_ref):
    cd = cd_ref[...]
    sd = sd_ref[...]
    dq = dq_ref[...]
    dk = dk_ref[...]
    dqo_ref[...] = (_rope(dq.astype(F32), dq, rd_ref, cd, sd) * (DIFF_HEAD_DIM ** -0.5 * LOG2E)).astype(BF16)
    dko_ref[...] = _rope(dk.astype(F32), dk, rd_ref, cd, sd).astype(BF16)
    _store_vt(dvt_ref, dv_ref[...], DIFF_HEADS)

    cqn = _rms(cq_ref[...].astype(F32), gq_ref[...]).astype(BF16)
    q = _dot(cqn, wuq_ref[...])
    qb = q.astype(BF16)
    for h in range(MLA_HEADS):
        hs = slice(h * MLA_QK_PAD, (h + 1) * MLA_QK_PAD)
        mq_ref[:, hs] = _rope(q[:, hs], qb[:, hs], rmq_ref, cmq_ref[:, hs], smq_ref[:, hs]).astype(BF16)

    ckvn = _rms(ckv_ref[...].astype(F32), gkv_ref[...]).astype(BF16)
    kv = _dot(ckvn, wukv_ref[...])
    ps = ps_ref[...]
    kpe = _rope(ps, ps.astype(BF16), rkp_ref, ckp_ref[...], skp_ref[...]).astype(BF16)
    for h in range(MLA_HEADS):
        mk_ref[:, h * MLA_QK_PAD:h * MLA_QK_PAD + LANES] = kv[:, h * LANES:(h + 1) * LANES].astype(BF16)
        mk_ref[:, h * MLA_QK_PAD + LANES:(h + 1) * MLA_QK_PAD] = kpe
    _store_vt(mvt_ref, kv[:, MLA_HEADS * LANES:].astype(BF16), MLA_HEADS)

    z = ps + bf_ref[...]
    lf = jnp.minimum(z, 0.0) - jnp.log1p(jnp.exp(-jnp.abs(z)))
    lft_ref[...] = lf.T[MLA_ROPE_DIM:MLA_ROPE_DIM + 8, :]


def _prep(pa, ps, tabs, gq, gkv, wuq, wukv, rmats, bf):
    s = pa.shape[0]
    tm = PREP_TM
    row = lambda w: pl.BlockSpec((tm, w), lambda i: (i, 0))
    col = lambda w, c: pl.BlockSpec((tm, w), lambda i: (i, c))
    tr = lambda r: pl.BlockSpec((r, tm), lambda i: (0, i))
    full = lambda a: pl.BlockSpec(a.shape, lambda i: (0,) * a.ndim)
    cd, sd, cmq, smq, ckp, skp = tabs
    rd, rmq, rkp = rmats
    return pl.pallas_call(
        _prep_kernel,
        grid=(s // tm,),
        in_specs=[col(512, 3), col(512, 4), col(512, 5), col(512, 6), col(512, 7), row(LANES),
                  row(512), row(512), row(1024), row(1024), row(LANES), row(LANES),
                  full(gq), full(gkv), full(wuq), full(wukv), full(rd), full(rmq), full(rkp), full(bf)],
        out_specs=[row(512), row(512), tr(4 * VT_ROWS), row(1024), row(1024), tr(4 * VT_ROWS), tr(8)],
        out_shape=[jax.ShapeDtypeStruct((s, 512), BF16), jax.ShapeDtypeStruct((s, 512), BF16),
                   jax.ShapeDtypeStruct((4 * VT_ROWS, s), BF16),
                   jax.ShapeDtypeStruct((s, 1024), BF16), jax.ShapeDtypeStruct((s, 1024), BF16),
                   jax.ShapeDtypeStruct((4 * VT_ROWS, s), BF16), jax.ShapeDtypeStruct((8, s), F32)],
        compiler_params=_cparams(1),
        name="prep",
    )(pa, pa, pa, pa, pa, ps, cd, sd, cmq, smq, ckp, skp, gq, gkv, wuq, wukv, rd, rmq, rkp, bf)


def _cumsum_kernel(x_ref, u_ref, o_ref, carry_ref):
    @pl.when(pl.program_id(0) == 0)
    def _():
        carry_ref[...] = jnp.zeros_like(carry_ref)

    a, b, c = _split3(x_ref[...])
    u = u_ref[...]
    cs = _dot(a, u) + _dot(b, u) + _dot(c, u) + carry_ref[:, :1]
    o_ref[...] = cs
    carry_ref[...] = jnp.broadcast_to(cs[:, CUM_T - 1:CUM_T], carry_ref.shape)


def _cumsum(x, u):
    s = x.shape[1]
    return pl.pallas_call(
        _cumsum_kernel,
        grid=(s // CUM_T,),
        in_specs=[pl.BlockSpec((8, CUM_T), lambda i: (0, i)), pl.BlockSpec(u.shape, lambda i: (0, 0))],
        out_specs=pl.BlockSpec((8, CUM_T), lambda i: (0, i)),
        out_shape=jax.ShapeDtypeStruct((8, s), F32),
        scratch_shapes=[pltpu.VMEM((8, LANES), F32)],
        compiler_params=_cparams(1),
        name="cumsum",
    )(x, u)


N_BIAS = 3


def _foxprep_kernel(q_ref, k_ref, v_ref, cum_ref, qo_ref, ko_ref, vt_ref):
    cum = cum_ref[...]
    beta = (cum[:, 0:1] - cum) * LOG2E
    pieces = [p.astype(F32).T for p in _split3(beta)]
    lane = lax.broadcasted_iota(jnp.int32, (ATT_T, LANES), 1)
    ones = jnp.where(lane < N_BIAS, 1.0, 0.0).astype(BF16)
    q = q_ref[...].astype(F32) * (FOX_HEAD_DIM ** -0.5 * LOG2E)
    for h in range(FOX_HEADS):
        e = jnp.zeros((ATT_T, LANES), F32)
        for n, p in enumerate(pieces):
            e = jnp.where(lane == n, p[:, h:h + 1], e)
        ko_ref[:, h * FOX_QK_PAD:h * FOX_QK_PAD + LANES] = k_ref[:, h * LANES:(h + 1) * LANES]
        ko_ref[:, h * FOX_QK_PAD + LANES:(h + 1) * FOX_QK_PAD] = e.astype(BF16)
        qo_ref[:, h * FOX_QK_PAD:h * FOX_QK_PAD + LANES] = q[:, h * LANES:(h + 1) * LANES].astype(BF16)
        qo_ref[:, h * FOX_QK_PAD + LANES:(h + 1) * FOX_QK_PAD] = ones
    _store_vt(vt_ref, v_ref[...], FOX_HEADS)


def _foxprep(pa, cum):
    s = pa.shape[0]
    t = ATT_T
    w = FOX_HEADS * FOX_QK_PAD
    return pl.pallas_call(
        _foxprep_kernel,
        grid=(s // t,),
        in_specs=[pl.BlockSpec((t, 512), lambda i: (i, 0)), pl.BlockSpec((t, 512), lambda i: (i, 1)),
                  pl.BlockSpec((t, 512), lambda i: (i, 2)), pl.BlockSpec((8, t), lambda i: (0, i))],
        out_specs=[pl.BlockSpec((t, w), lambda i: (i, 0)), pl.BlockSpec((t, w), lambda i: (i, 0)),
                   pl.BlockSpec((FOX_HEADS * VT_ROWS, t), lambda i: (0, i))],
        out_shape=[jax.ShapeDtypeStruct((s, w), BF16), jax.ShapeDtypeStruct((s, w), BF16),
                   jax.ShapeDtypeStruct((FOX_HEADS * VT_ROWS, s), BF16)],
        compiler_params=_cparams(1),
        name="foxprep",
    )(pa, pa, pa, cum)


def _flash_kernel(*refs, mode, lam_init):
    t = ATT_T
    if mode == "fox":
        q_ref, k_ref, vt_ref, ck_ref, o_ref, sa_ref, sb_ref, xa_ref, xb_ref, m_ref, acc_ref = refs
    elif mode == "diff":
        q_ref, k_ref, vt_ref, lam_ref, gd_ref, o_ref, sa_ref, sb_ref, xa_ref, xb_ref, m_ref, acc_ref = refs
    else:
        q_ref, k_ref, vt_ref, o_ref, sa_ref, sb_ref, xa_ref, xb_ref, m_ref, acc_ref = refs
    i = pl.program_id(1)
    off_i = pl.multiple_of(i * ATT_TQ, ATT_TQ)
    m_ref[...] = jnp.full(m_ref.shape, NEG, F32)
    acc_ref[...] = jnp.zeros_like(acc_ref)

    q = q_ref[...]
    if mode == "diff":
        lane = lax.broadcasted_iota(jnp.int32, q.shape, 1)
        zero = jnp.zeros_like(q)
        qs = [jnp.where(lane < DIFF_HEAD_DIM, q, zero), jnp.where(lane >= DIFF_HEAD_DIM, q, zero)]
    else:
        qs = [q]

    buf_a = (sa_ref, xa_ref)
    buf_b = (sb_ref, xb_ref)

    def scores(off, buf, q0=0):
        s_ref, x_ref = buf
        w = ATT_TQ - q0
        kb = k_ref[pl.ds(off, t), :]
        for n, qn in enumerate(qs):
            st = _dot_nt(kb, qn[q0:])
            s_ref[n, :, :w] = st
            x_ref[n, :, :w] = jnp.max(st, axis=0, keepdims=True)

    def consume(off, buf, masked, q0=0):
        s_ref, x_ref = buf
        w = ATT_TQ - q0
        vtb = vt_ref[:, pl.ds(off, t)]
        for n in range(len(qs)):
            st = s_ref[n, :, :w]
            if masked:
                key_id = lax.broadcasted_iota(jnp.int32, st.shape, 0) + (off - off_i)
                qry_id = lax.broadcasted_iota(jnp.int32, st.shape, 1) + q0
                st = jnp.where(key_id <= qry_id, st, NEG)
                m = jnp.max(st, axis=0, keepdims=True)
            else:
                m = x_ref[n, :, :w]
            p = jnp.exp2((st - m).astype(BF16))
            o = _dot(vtb, p)
            if mode == "fox":
                m = m + (ck_ref[:, pl.ds(off_i, LANES)][:, :1] - ck_ref[:, pl.ds(off, LANES)][:, :1]) * LOG2E
            m_prev = m_ref[n, :, q0:]
            m_new = jnp.maximum(m_prev, m)
            acc_ref[n, :, q0:] = jnp.exp2(m_prev - m_new) * acc_ref[n, :, q0:] + jnp.exp2(m - m_new) * o
            m_ref[n, :, q0:] = m_new

    scores(0, buf_a)

    def pair(jj, carry):
        off0 = pl.multiple_of(jj * (2 * t), 2 * t)
        off1 = pl.multiple_of(off0 + t, t)
        scores(off1, buf_b)
        consume(off0, buf_a, False)
        scores(pl.multiple_of(off1 + t, t), buf_a)
        consume(off1, buf_b, False)
        return carry

    lax.fori_loop(0, i, pair, 0)
    off_d = pl.multiple_of(off_i + t, t)
    scores(off_d, buf_b, q0=t)
    consume(off_i, buf_a, True)
    consume(off_d, buf_b, True, q0=t)

    def normalised(n):
        acc = acc_ref[n]
        return acc[:LANES] / acc[LANES:LANES + 1]

    if mode == "diff":
        dl = lam_ref[...]
        lam = (jnp.exp(jnp.sum(dl[0:1] * dl[1:2], axis=1, keepdims=True))
               - jnp.exp(jnp.sum(dl[2:3] * dl[3:4], axis=1, keepdims=True)) + lam_init)
        o = _rms((normalised(0) - lam * normalised(1)).T, gd_ref[...]) * (1.0 - lam_init)
    else:
        o = normalised(0).T
    o_ref[...] = o.astype(o_ref.dtype)


def _flash(mode, q_arr, k_arr, vt_arr, dk, heads, extra=(), lam_init=0.0):
    s = q_arr.shape[0]
    t = ATT_T
    tq = ATT_TQ
    dv = LANES
    n_stream = 2 if mode == "diff" else 1
    in_specs = [pl.BlockSpec((tq, dk), lambda h, i: (i, h)),
                pl.BlockSpec((s, dk), lambda h, i: (0, h)),
                pl.BlockSpec((VT_ROWS, s), lambda h, i: (h, 0))]
    if mode == "fox":
        in_specs.append(pl.BlockSpec((None, 1, s), lambda h, i: (h, 0, 0)))
    elif mode == "diff":
        in_specs += [pl.BlockSpec(extra[0].shape, lambda h, i: (0, 0)),
                     pl.BlockSpec(extra[1].shape, lambda h, i: (0, 0))]
    return pl.pallas_call(
        functools.partial(_flash_kernel, mode=mode, lam_init=lam_init),
        grid=(heads, s // tq),
        in_specs=in_specs,
        out_specs=pl.BlockSpec((tq, dv), lambda h, i: (i, h)),
        out_shape=jax.ShapeDtypeStruct((s, heads * dv), BF16),
        scratch_shapes=[pltpu.VMEM((n_stream, t, tq), F32), pltpu.VMEM((n_stream, t, tq), F32),
                        pltpu.VMEM((n_stream, 1, tq), F32), pltpu.VMEM((n_stream, 1, tq), F32),
                        pltpu.VMEM((n_stream, 1, tq), F32), pltpu.VMEM((n_stream, VT_ROWS, tq), F32)],
        compiler_params=_cparams(2),
        name="flash_" + mode,
    )(q_arr, k_arr, vt_arr, *extra)


def _layer_norm(z, g, b):
    mu = jnp.mean(z, axis=-1, keepdims=True)
    zc = z - mu
    var = jnp.mean(jnp.square(zc), axis=-1, keepdims=True)
    return zc * lax.rsqrt(var + NORM_EPS) * g + b


def _first_index_of_max(vals):
    best = vals[0]
    for v in vals[1:]:
        best = jnp.maximum(best, v)
    idx = jnp.full(best.shape, len(vals) - 1, jnp.int32)
    for j in range(len(vals) - 2, -1, -1):
        idx = jnp.where(vals[j] == best, j, idx)
    return best, idx


def _post_kernel(of_ref, od_ref, om_ref, g0_ref, g1_ref, g2_ref, x_ref, wf_ref, wd_ref, wm_ref, wo_ref,
                 lg_ref, lb_ref, wr_ref, rb_ref, x1_ref, h16_ref, eid_ref, wt_ref):
    tm = POST_TM
    merged = (g0_ref[...].astype(F32) * _dot(of_ref[...], wf_ref[...])
              + g1_ref[...].astype(F32) * _dot(od_ref[...], wd_ref[...])
              + g2_ref[...].astype(F32) * _dot(om_ref[...], wm_ref[...]))
    mix = _dot(merged.astype(BF16), wo_ref[...])
    x1 = _layer_norm(DEEPNORM_ALPHA * x_ref[...] + mix, lg_ref[...], lb_ref[...])
    x1_ref[...] = x1
    for c in range(ROW_CHUNKS):
        h16_ref[pl.ds(c, tm, stride=ROW_CHUNKS), :] = x1[:, c * LANES:(c + 1) * LANES]

    e = N_EXPERTS
    prod = [_dot(p, wr_ref[...]).T for p in _split3(x1)]
    logits = (prod[0][0:e] + (prod[0][e:2 * e] + prod[1][0:e])
              + (prod[0][2 * e:3 * e] + prod[1][e:2 * e] + prod[2][0:e]))
    scores = jax.nn.sigmoid(logits)
    biased = scores + rb_ref[...]
    a = [biased[j * 8:(j + 1) * 8] for j in range(EXPERTS_PER_GROUP)]
    u = [scores[j * 8:(j + 1) * 8] for j in range(EXPERTS_PER_GROUP)]
    gs = a[0] + a[1]
    for j0 in range(EXPERTS_PER_GROUP):
        for j1 in range(j0 + 1, EXPERTS_PER_GROUP):
            gs = jnp.maximum(gs, a[j0] + a[j1])
    gmax = jnp.max(gs, axis=0, keepdims=True)
    gid = lax.broadcasted_iota(jnp.int32, gs.shape, 0)
    best = jnp.min(jnp.where(gs == gmax, gid, N_GROUPS), axis=0, keepdims=True)
    sel = gid == best
    ing = [jnp.sum(jnp.where(sel, a[j], 0.0), axis=0, keepdims=True) for j in range(EXPERTS_PER_GROUP)]
    unb = [jnp.sum(jnp.where(sel, u[j], 0.0), axis=0, keepdims=True) for j in range(EXPERTS_PER_GROUP)]
    _, l1 = _first_index_of_max(ing)
    ing2 = [jnp.where(l1 == j, -jnp.inf, ing[j]) for j in range(EXPERTS_PER_GROUP)]
    _, l2 = _first_index_of_max(ing2)
    w1 = jnp.zeros_like(unb[0])
    w2 = jnp.zeros_like(unb[0])
    for j in range(EXPERTS_PER_GROUP):
        w1 = jnp.where(l1 == j, unb[j], w1)
        w2 = jnp.where(l2 == j, unb[j], w2)
    tot = w1 + w2
    eid_ref[...] = jnp.zeros_like(eid_ref)
    wt_ref[...] = jnp.zeros_like(wt_ref)
    eid_ref[0:1, :] = best * EXPERTS_PER_GROUP + l1
    eid_ref[1:2, :] = best * EXPERTS_PER_GROUP + l2
    wt_ref[0:1, :] = w1 / tot
    wt_ref[1:2, :] = w2 / tot


def _post(o_fox, o_diff, o_mla, gates, x, wf, wd, wm, wo, lg, lb, wr, rb):
    s = x.shape[0]
    tm = POST_TM
    row = lambda w: pl.BlockSpec((tm, w), lambda i: (i, 0))
    full = lambda a: pl.BlockSpec(a.shape, lambda i: (0,) * a.ndim, pipeline_mode=pl.Buffered(1))
    return pl.pallas_call(
        _post_kernel,
        grid=(s // tm,),
        in_specs=[row(512), row(512), row(512),
                  pl.BlockSpec((tm, D_MODEL), lambda i: (i, 0)),
                  pl.BlockSpec((tm, D_MODEL), lambda i: (i, 1)),
                  pl.BlockSpec((tm, D_MODEL), lambda i: (i, 2)),
                  row(D_MODEL), full(wf), full(wd), full(wm), full(wo), full(lg), full(lb), full(wr), full(rb)],
        out_specs=[row(D_MODEL), pl.BlockSpec((tm * ROW_CHUNKS, LANES), lambda i: (i, 0)),
                   pl.BlockSpec((8, tm), lambda i: (0, i)), pl.BlockSpec((8, tm), lambda i: (0, i))],
        out_shape=[jax.ShapeDtypeStruct((s, D_MODEL), F32),
                   jax.ShapeDtypeStruct((s * ROW_CHUNKS, LANES), F32),
                   jax.ShapeDtypeStruct((8, s), jnp.int32), jax.ShapeDtypeStruct((8, s), F32)],
        compiler_params=_cparams(1),
        name="post",
    )(o_fox, o_diff, o_mla, gates, gates, gates, x, wf, wd, wm, wo, lg, lb, wr, rb)


def _rank_kernel(eid_ref, us_ref, rank_ref, cnt_ref, carry_ref):
    @pl.when(pl.program_id(0) == 0)
    def _():
        carry_ref[...] = jnp.zeros_like(carry_ref)

    e0 = eid_ref[0:1, :]
    e1 = eid_ref[1:2, :]
    eio = lax.broadcasted_iota(jnp.int32, (N_EXPERTS, RANK_T), 0)
    hit0 = eio == e0
    hit1 = eio == e1
    onehot = jnp.where(hit0 | hit1, 1.0, 0.0)
    before = _dot(onehot.astype(BF16), us_ref[...]) + carry_ref[:, :1]
    rank_ref[...] = jnp.zeros_like(rank_ref)
    rank_ref[0:1, :] = jnp.sum(jnp.where(hit0, before, 0.0), axis=0, keepdims=True).astype(jnp.int32)
    rank_ref[1:2, :] = jnp.sum(jnp.where(hit1, before, 0.0), axis=0, keepdims=True).astype(jnp.int32)
    carry_ref[...] = carry_ref[...] + jnp.sum(onehot, axis=1, keepdims=True)
    cnt_ref[...] = carry_ref[...]


def _rank(eid, us):
    s = eid.shape[1]
    return pl.pallas_call(
        _rank_kernel,
        grid=(s // RANK_T,),
        in_specs=[pl.BlockSpec((8, RANK_T), lambda i: (0, i)), pl.BlockSpec(us.shape, lambda i: (0, 0))],
        out_specs=[pl.BlockSpec((8, RANK_T), lambda i: (0, i)),
                   pl.BlockSpec((N_EXPERTS, LANES), lambda i: (0, 0))],
        out_shape=[jax.ShapeDtypeStruct((8, s), jnp.int32), jax.ShapeDtypeStruct((N_EXPERTS, LANES), F32)],
        scratch_shapes=[pltpu.VMEM((N_EXPERTS, LANES), F32)],
        compiler_params=_cparams(1),
        name="rank",
    )(eid, us)


def _dest_kernel(cnt_ref, eid_ref, rank_ref, lt_ref, dest_ref, blk_ref):
    nb = jnp.floor((cnt_ref[...] + (MOE_R - 1)) * (1.0 / MOE_R))
    incl = _dot(lt_ref[...], nb.astype(BF16))
    start = (incl - nb)[:, :1] * MOE_R
    eio = lax.broadcasted_iota(jnp.int32, (N_EXPERTS, DEST_T), 0)
    dest_ref[...] = jnp.zeros_like(dest_ref)
    for k in range(TOP_K):
        base = jnp.sum(jnp.where(eio == eid_ref[k:k + 1, :], start, 0.0), axis=0, keepdims=True)
        dest_ref[k:k + 1, :] = base.astype(jnp.int32) + rank_ref[k:k + 1, :]
    bid = lax.broadcasted_iota(jnp.int32, (N_EXPERTS, BLK_LANES), 1).astype(F32)
    ended = jnp.sum(jnp.where(incl[:, :1] <= bid, 1.0, 0.0), axis=0, keepdims=True)
    blk_ref[...] = jnp.zeros_like(blk_ref)
    blk_ref[0:1, :] = jnp.minimum(ended, N_EXPERTS - 1.0).astype(jnp.int32)
    blk_ref[1:2, :] = jnp.broadcast_to(incl[N_EXPERTS - 1:N_EXPERTS, :1], (1, BLK_LANES)).astype(jnp.int32)


def _dest(cnt, eid, rank, lt):
    s = eid.shape[1]
    return pl.pallas_call(
        _dest_kernel,
        grid=(s // DEST_T,),
        in_specs=[pl.BlockSpec(cnt.shape, lambda i: (0, 0)),
                  pl.BlockSpec((8, DEST_T), lambda i: (0, i)), pl.BlockSpec((8, DEST_T), lambda i: (0, i)),
                  pl.BlockSpec(lt.shape, lambda i: (0, 0))],
        out_specs=[pl.BlockSpec((8, DEST_T), lambda i: (0, i)), pl.BlockSpec((8, BLK_LANES), lambda i: (0, 0))],
        out_shape=[jax.ShapeDtypeStruct((8, s), jnp.int32), jax.ShapeDtypeStruct((8, BLK_LANES), jnp.int32)],
        compiler_params=_cparams(1),
        name="dest",
    )(cnt, eid, rank, lt)


def _row_copy(src_ref, src_row, dst_ref, dst_row, sem):
    def rows(row):
        start = row * ROW_CHUNKS
        return pl.ds(start if isinstance(row, int) else pl.multiple_of(start, ROW_CHUNKS), ROW_CHUNKS)

    return pltpu.make_async_copy(src_ref.at[rows(src_row)], dst_ref.at[rows(dst_row)], sem)


def _dispatch_kernel(dest_ref, h_ref, xs_in_ref, xs_ref, sem):
    del xs_in_ref
    base = pl.program_id(0) * DISP_T

    def issue(r, carry):
        for k in range(TOP_K):
            _row_copy(h_ref, r, xs_ref, dest_ref[k * SEQ + base + r], sem).start(priority=k)
        return carry

    lax.fori_loop(0, DISP_T, issue, 0, unroll=ISSUE_UNROLL)
    for k in range(TOP_K):
        pltpu.make_async_copy(h_ref, xs_ref.at[pl.ds(0, DISP_T * ROW_CHUNKS)], sem).wait()


def _dispatch(dest_flat, h16, xs_zero):
    return pl.pallas_call(
        _dispatch_kernel,
        grid_spec=pltpu.PrefetchScalarGridSpec(
            num_scalar_prefetch=1, grid=(SEQ // DISP_T,),
            in_specs=[pl.BlockSpec((DISP_T * ROW_CHUNKS, LANES), lambda i, d: (i, 0)),
                      pl.BlockSpec(memory_space=pl.ANY)],
            out_specs=pl.BlockSpec(memory_space=pl.ANY),
            scratch_shapes=[pltpu.SemaphoreType.DMA(())]),
        out_shape=jax.ShapeDtypeStruct(xs_zero.shape, xs_zero.dtype),
        input_output_aliases={2: 0},
        compiler_params=_cparams(1),
        name="dispatch",
    )(dest_flat, h16, xs_zero)


def _moe_kernel(be_ref, nu_ref, xs_ref, wgu_ref, wout_ref, y_ref, lhs_ref):
    del be_ref
    used = pl.program_id(0) < nu_ref[0]

    @pl.when(jnp.logical_not(used))
    def _():
        y_ref[...] = jnp.zeros_like(y_ref)

    @pl.when(used)
    def _():
        for c in range(ROW_CHUNKS):
            lhs_ref[:, c * LANES:(c + 1) * LANES] = xs_ref[pl.ds(c, MOE_R, stride=ROW_CHUNKS), :].astype(BF16)
        gu = _dot(lhs_ref[...], wgu_ref[...])
        hmid = (jax.nn.silu(gu[:, :D_FF_EXPERT]) * gu[:, D_FF_EXPERT:]).astype(BF16)
        y = _dot(hmid, wout_ref[...])
        for c in range(ROW_CHUNKS):
            y_ref[pl.ds(c, MOE_R, stride=ROW_CHUNKS), :] = y[:, c * LANES:(c + 1) * LANES]


def _moe(blk_e, n_used, xs, wgu, wout, layer):
    last = lambda b, nu: jnp.minimum(b, nu[0] - 1)
    rows = pl.BlockSpec((MOE_R * ROW_CHUNKS, LANES), lambda b, be, nu: (last(b, nu), 0))
    return pl.pallas_call(
        _moe_kernel,
        grid_spec=pltpu.PrefetchScalarGridSpec(
            num_scalar_prefetch=2, grid=(MOE_NB,),
            in_specs=[rows,
                      pl.BlockSpec((None, None) + wgu.shape[2:], lambda b, be, nu: (layer, be[last(b, nu)], 0, 0)),
                      pl.BlockSpec((None, None) + wout.shape[2:], lambda b, be, nu: (layer, be[last(b, nu)], 0, 0))],
            out_specs=pl.BlockSpec((MOE_R * ROW_CHUNKS, LANES), lambda b, be, nu: (b, 0)),
            scratch_shapes=[pltpu.VMEM((MOE_R, D_MODEL), BF16)]),
        out_shape=jax.ShapeDtypeStruct(xs.shape, F32),
        compiler_params=_cparams(1),
        name="moe",
    )(blk_e, n_used, xs, wgu, wout)


def _combine_kernel(dest_ref, y_ref, x1_ref, wt_ref, g_ref, b_ref, o_ref, buf_ref, sem):
    i = pl.program_id(0)
    slot = i & 1

    def issue(tile, sl):
        def body(r, carry):
            for k in range(TOP_K):
                _row_copy(y_ref, dest_ref[k * SEQ + tile * COMB_T + r], buf_ref.at[sl, k], r,
                          sem.at[sl]).start(priority=k)
            return carry

        lax.fori_loop(0, COMB_T, body, 0, unroll=ISSUE_UNROLL)

    @pl.when(i == 0)
    def _():
        issue(0, 0)

    @pl.when(i + 1 < pl.num_programs(0))
    def _():
        issue(i + 1, 1 - slot)

    for k in range(TOP_K):
        pltpu.make_async_copy(y_ref.at[pl.ds(0, COMB_T * ROW_CHUNKS)], buf_ref.at[slot, k], sem.at[slot]).wait()

    wcol = wt_ref[...].T
    w0 = wcol[:, 0:1]
    w1 = wcol[:, 1:2]
    ffn = jnp.concatenate(
        [w0 * buf_ref[slot, 0, pl.ds(c, COMB_T, stride=ROW_CHUNKS), :]
         + w1 * buf_ref[slot, 1, pl.ds(c, COMB_T, stride=ROW_CHUNKS), :] for c in range(ROW_CHUNKS)], axis=1)
    o_ref[...] = _layer_norm(DEEPNORM_ALPHA * x1_ref[...] + ffn, g_ref[...], b_ref[...])


def _combine(dest_flat, y16, x1, wt, g, b):
    tm = COMB_T
    return pl.pallas_call(
        _combine_kernel,
        grid_spec=pltpu.PrefetchScalarGridSpec(
            num_scalar_prefetch=1, grid=(SEQ // tm,),
            in_specs=[pl.BlockSpec(memory_space=pl.ANY),
                      pl.BlockSpec((tm, D_MODEL), lambda i, d: (i, 0)),
                      pl.BlockSpec((8, tm), lambda i, d: (0, i)),
                      pl.BlockSpec(g.shape, lambda i, d: (0, 0)), pl.BlockSpec(b.shape, lambda i, d: (0, 0))],
            out_specs=pl.BlockSpec((tm, D_MODEL), lambda i, d: (i, 0)),
            scratch_shapes=[pltpu.VMEM((2, TOP_K, tm * ROW_CHUNKS, LANES), F32), pltpu.SemaphoreType.DMA((2,))]),
        out_shape=jax.ShapeDtypeStruct(x1.shape, F32),
        compiler_params=_cparams(1),
        name="combine",
    )(dest_flat, y16, x1, wt, g, b)


def _rot_matrix(width, groups, half):
    r = np.zeros((width, width), np.float32)
    for o in groups:
        for j in range(half):
            r[o + j + half, o + j] = -1.0
            r[o + j, o + half + j] = 1.0
    return jnp.asarray(r, BF16)


def _rope_tables():
    pos = jnp.arange(SEQ, dtype=F32)

    def cs(dim):
        inv_freq = ROPE_THETA ** (-jnp.arange(0, dim, 2, dtype=F32) / dim)
        ang = pos[:, None] * inv_freq[None, :]
        return jnp.cos(ang), jnp.sin(ang)

    cp, sp = cs(DIFF_ROT_DIM)
    one = jnp.ones((SEQ, DIFF_HEAD_DIM - DIFF_ROT_DIM), F32)
    cd = jnp.tile(jnp.concatenate([cp, cp, one], axis=1), (1, 2 * DIFF_HEADS))
    sd = jnp.tile(jnp.concatenate([sp, sp, 0.0 * one], axis=1), (1, 2 * DIFF_HEADS))
    cm, sm = cs(MLA_ROPE_DIM)
    scale = MLA_QK_DIM ** -0.5 * LOG2E
    ones_n = jnp.ones((SEQ, MLA_NOPE_DIM), F32)
    pad = jnp.zeros((SEQ, MLA_QK_PAD - MLA_QK_DIM), F32)
    cmq = jnp.tile(jnp.concatenate([ones_n, cm, cm, pad], axis=1) * scale, (1, MLA_HEADS))
    smq = jnp.tile(jnp.concatenate([0.0 * ones_n, sm, sm, pad], axis=1) * scale, (1, MLA_HEADS))
    ckp = jnp.concatenate([cm, cm, pad], axis=1)
    skp = jnp.concatenate([sm, sm, pad], axis=1)
    return cd, sd, cmq, smq, ckp, skp


def _constants():
    rd = _rot_matrix(512, [c * DIFF_HEAD_DIM for c in range(2 * DIFF_HEADS)], DIFF_ROT_DIM // 2)
    rmq = _rot_matrix(MLA_QK_PAD, [MLA_NOPE_DIM], MLA_ROPE_DIM // 2)
    rkp = _rot_matrix(LANES, [0], MLA_ROPE_DIM // 2)
    tri = np.arange(CUM_T)
    u_incl = jnp.asarray(tri[:, None] <= tri[None, :], BF16)
    tri = np.arange(RANK_T)
    u_strict = jnp.asarray(tri[:, None] < tri[None, :], BF16)
    tri = np.arange(N_EXPERTS)
    l_incl = jnp.asarray(tri[:, None] >= tri[None, :], BF16)
    return (rd, rmq, rkp), u_incl, u_strict, l_incl


def kernel(x, w_in, b_fox_f, b_gate, diff_lambda, g_diff, g_mla_q, g_mla_kv, w_mla_uq, w_mla_ukv, w_fox_up, w_diff_up, w_mla_up, w_o, ln1_g, ln1_b, ln2_g, ln2_b, w_router, router_bias, w_exp_in, w_exp_out):
    s = x.shape[1]
    assert x.shape == (1, SEQ, D_MODEL) and w_in.shape == (DEPTH, D_MODEL, IN_WIDTH)
    xf = x.reshape(s, D_MODEL)
    tabs = _rope_tables()
    rmats, u_incl, u_strict, l_incl = _constants()
    w_a, w_s, w_g = _repack(w_in)

    perm = np.array([g * EXPERTS_PER_GROUP + j for j in range(EXPERTS_PER_GROUP) for g in range(N_GROUPS)])
    wr = jnp.concatenate(list(_split3(w_router[:, perm]))
                         + [jnp.zeros((D_MODEL, LANES - 3 * N_EXPERTS), BF16)], axis=1)
    rb = router_bias[perm].reshape(N_EXPERTS, 1)
    wgu = w_exp_in.astype(BF16)
    wout = w_exp_out.astype(BF16)
    zero_a = jnp.zeros((1, W_ATTN), F32)
    zero_s = jnp.zeros((1, LANES), F32)

    for i in range(DEPTH):
        pa = _linear(xf, w_a, i, zero_a, None, BF16, LIN_TN, "inproj_attn")
        ps = _linear(xf, w_s, i, zero_s, None, F32, LANES, "inproj_small")
        gates = _linear(xf, w_g, i, b_gate[i].reshape(1, -1), "sigmoid", BF16, LIN_TN, "inproj_gate")

        uq = w_mla_uq[i].reshape(MLA_Q_RANK, MLA_HEADS, MLA_QK_DIM)
        wuq = jnp.concatenate([uq, jnp.zeros((MLA_Q_RANK, MLA_HEADS, MLA_QK_PAD - MLA_QK_DIM), F32)], axis=2)
        wuq = wuq.reshape(MLA_Q_RANK, MLA_HEADS * MLA_QK_PAD).astype(BF16)
        ukv = w_mla_ukv[i].reshape(MLA_KV_RANK, MLA_HEADS, MLA_NOPE_DIM + MLA_V_DIM)
        wukv = jnp.concatenate([ukv[:, :, :MLA_NOPE_DIM].reshape(MLA_KV_RANK, -1),
                                ukv[:, :, MLA_NOPE_DIM:].reshape(MLA_KV_RANK, -1)], axis=1).astype(BF16)
        bf = jnp.zeros((1, LANES), F32).at[0, MLA_ROPE_DIM:MLA_ROPE_DIM + FOX_HEADS].set(b_fox_f[i])

        dq, dk, dvt, mq, mk, mvt, lft = _prep(pa, ps, tabs, g_mla_q[i].reshape(1, -1), g_mla_kv[i].reshape(1, -1),
                                              wuq, wukv, rmats, bf)
        cum = _cumsum(lft, u_incl)
        fq, fk, fvt = _foxprep(pa, cum)
        cum_row = cum[:FOX_HEADS].reshape(FOX_HEADS, 1, s)

        lam_init = 0.8 - 0.6 * math.exp(-0.3 * i)
        o_fox = _flash("fox", fq, fk, fvt, FOX_QK_PAD, FOX_HEADS, extra=(cum_row,))
        o_diff = _flash("diff", dq, dk, dvt, 2 * DIFF_HEAD_DIM, DIFF_HEADS,
                        extra=(diff_lambda[i], g_diff[i].reshape(1, -1)), lam_init=lam_init)
        o_mla = _flash("mla", mq, mk, mvt, MLA_QK_PAD, MLA_HEADS)

        x1, h16, eid, wt = _post(o_fox, o_diff, o_mla, gates, xf,
                                 w_fox_up[i].astype(BF16), w_diff_up[i].astype(BF16), w_mla_up[i].astype(BF16),
                                 w_o[i].astype(BF16), ln1_g[i].reshape(1, -1), ln1_b[i].reshape(1, -1), wr, rb)

        rank, cnt = _rank(eid, u_strict)
        dest, blk = _dest(cnt, eid, rank, l_incl)
        dest_flat = dest[:TOP_K].reshape(TOP_K * s)
        xs = _dispatch(dest_flat, h16, jnp.zeros((MOE_ROWS * ROW_CHUNKS, LANES), F32))
        y16 = _moe(blk[0], blk[1, :1], xs, wgu, wout, i)
        xf = _combine(dest_flat, y16, x1, wt, ln2_g[i].reshape(1, -1), ln2_b[i].reshape(1, -1))
    return xf.reshape(1, s, D_MODEL)
```

```python
import functools
import math

import numpy as np
import jax
import jax.numpy as jnp
from jax import lax
from jax.experimental import pallas as pl
from jax.experimental.pallas import tpu as pltpu

F32 = jnp.float32
BF16 = jnp.bfloat16

D_MODEL = 2048
SEQ = 16384
DEPTH = 2
ROPE_THETA = 500000.0
NORM_EPS = 1e-5
FOX_HEADS = 4
FOX_HEAD_DIM = 128
DIFF_HEADS = 4
DIFF_HEAD_DIM = 64
DIFF_V_DIM = 128
DIFF_ROT_DIM = 16
MLA_HEADS = 4
MLA_Q_RANK = 512
MLA_KV_RANK = 512
MLA_NOPE_DIM = 128
MLA_ROPE_DIM = 64
MLA_V_DIM = 128
MLA_QK_DIM = MLA_NOPE_DIM + MLA_ROPE_DIM
N_EXPERTS = 32
N_GROUPS = 8
EXPERTS_PER_GROUP = 4
TOP_K = 2
D_FF_EXPERT = 1408
DEEPNORM_ALPHA = (2 * DEPTH) ** 0.25

LANES = 128
ROW_CHUNKS = D_MODEL // LANES
MLA_QK_PAD = 256
FOX_QK_PAD = 256
BF16_ROWS = 16
VT_ROWS = LANES + BF16_ROWS
LOG2E = math.log2(math.e)
VMEM_LIMIT = 56 * 1024 * 1024

LIN_TM = 1024
LIN_TN = 1024
PREP_TM = 512
CUM_T = 512
ATT_T = 512
ATT_TQ = 2 * ATT_T
POST_TM = 256
RANK_T = 512
DEST_T = 2048
DISP_T = 512
MOE_R = 256
MOE_NB = (SEQ * TOP_K + N_EXPERTS * (MOE_R - 1) + MOE_R - 1) // MOE_R
MOE_ROWS = MOE_NB * MOE_R
BLK_LANES = 256
COMB_T = 256
ISSUE_UNROLL = 8
NEG = -1e30


def _cparams(n_axes, **kw):
    return pltpu.CompilerParams(dimension_semantics=("arbitrary",) * n_axes,
                                vmem_limit_bytes=VMEM_LIMIT, **kw)


def _dot(a, b):
    return jnp.dot(a, b, preferred_element_type=F32)


def _dot_nt(a, b):
    return lax.dot_general(a, b, (((1,), (1,)), ((), ())), preferred_element_type=F32)


def _split3(x):
    a = x.astype(BF16)
    r = x - a.astype(F32)
    b = r.astype(BF16)
    c = (r - b.astype(F32)).astype(BF16)
    return a, b, c


def _linear_kernel(x_ref, w_ref, b_ref, o_ref, xb_ref, *, act):
    @pl.when(pl.program_id(1) == 0)
    def _():
        xb_ref[...] = x_ref[...].astype(BF16)

    acc = _dot(xb_ref[...], w_ref[...]) + b_ref[...]
    if act == "sigmoid":
        acc = jax.nn.sigmoid(acc)
    o_ref[...] = acc.astype(o_ref.dtype)


def _linear(x, w, layer, b, act, out_dtype, tn, name):
    m, k = x.shape
    n = w.shape[2]
    return pl.pallas_call(
        functools.partial(_linear_kernel, act=act),
        grid=(m // LIN_TM, n // tn),
        in_specs=[pl.BlockSpec((LIN_TM, k), lambda i, j: (i, 0)),
                  pl.BlockSpec((None, k, tn), lambda i, j: (layer, 0, j)),
                  pl.BlockSpec((1, tn), lambda i, j: (0, j))],
        out_specs=pl.BlockSpec((LIN_TM, tn), lambda i, j: (i, j)),
        out_shape=jax.ShapeDtypeStruct((m, n), out_dtype),
        scratch_shapes=[pltpu.VMEM((LIN_TM, k), BF16)],
        compiler_params=_cparams(2),
        name=name,
    )(x, w, b)


O_FOX_F = 3 * 512
O_DIFF = O_FOX_F + FOX_HEADS
O_MLA = O_DIFF + 3 * 512
O_KPE = O_MLA + 2 * 512
O_GATE = O_KPE + MLA_ROPE_DIM
IN_WIDTH = O_GATE + 3 * D_MODEL
W_ATTN = 4096
REPACK_TK = 256


def _repack_kernel(w_ref, wa_ref, ws_ref):
    wa_ref[:, 0:O_FOX_F] = w_ref[:, 0:O_FOX_F].astype(BF16)
    wa_ref[:, O_FOX_F:W_ATTN] = w_ref[:, O_DIFF:O_KPE].astype(BF16)
    wa_ref[:, W_ATTN:] = w_ref[:, O_GATE:IN_WIDTH].astype(BF16)
    ws_ref[...] = jnp.zeros_like(ws_ref)
    ws_ref[:, 0:MLA_ROPE_DIM] = w_ref[:, O_KPE:O_GATE].astype(BF16)
    ws_ref[:, MLA_ROPE_DIM:MLA_ROPE_DIM + FOX_HEADS] = w_ref[:, O_FOX_F:O_DIFF].astype(BF16)


def _repack(w_in):
    k = w_in.shape[1]
    blk = lambda n: pl.BlockSpec((None, REPACK_TK, n), lambda l, i: (l, i, 0))
    return pl.pallas_call(
        _repack_kernel,
        grid=(DEPTH, k // REPACK_TK),
        in_specs=[blk(IN_WIDTH)],
        out_specs=[blk(W_ATTN + 3 * D_MODEL), blk(LANES)],
        out_shape=[jax.ShapeDtypeStruct((DEPTH, k, W_ATTN + 3 * D_MODEL), BF16),
                   jax.ShapeDtypeStruct((DEPTH, k, LANES), BF16)],
        compiler_params=_cparams(2),
        name="repack",
    )(w_in)


N_ATTN_TILES = W_ATTN // LIN_TN


def _inproj_kernel(x_ref, w_ref, b_ref, pa_ref, g_ref, xb_ref):
    j = pl.program_id(1)

    @pl.when(j == 0)
    def _():
        xb_ref[...] = x_ref[...].astype(BF16)

    acc = _dot(xb_ref[...], w_ref[...])

    @pl.when(j < N_ATTN_TILES)
    def _():
        pa_ref[...] = acc.astype(BF16)

    @pl.when(j >= N_ATTN_TILES)
    def _():
        g_ref[...] = jax.nn.sigmoid(acc + b_ref[...]).astype(BF16)


def _inproj(x, w, layer, b_gate):
    m, k = x.shape
    n = w.shape[2]
    gate_tile = lambda j: jnp.maximum(j - N_ATTN_TILES, 0)
    return pl.pallas_call(
        _inproj_kernel,
        grid=(m // LIN_TM, n // LIN_TN),
        in_specs=[pl.BlockSpec((LIN_TM, k), lambda i, j: (i, 0)),
                  pl.BlockSpec((None, k, LIN_TN), lambda i, j: (layer, 0, j)),
                  pl.BlockSpec((1, LIN_TN), lambda i, j: (0, gate_tile(j)))],
        out_specs=[pl.BlockSpec((LIN_TM, LIN_TN), lambda i, j: (i, jnp.minimum(j, N_ATTN_TILES - 1))),
                   pl.BlockSpec((LIN_TM, LIN_TN), lambda i, j: (i, gate_tile(j)))],
        out_shape=[jax.ShapeDtypeStruct((m, W_ATTN), BF16), jax.ShapeDtypeStruct((m, n - W_ATTN), BF16)],
        scratch_shapes=[pltpu.VMEM((LIN_TM, k), BF16)],
        compiler_params=_cparams(2),
        name="inproj",
    )(x, w, b_gate)


def _rms(xf, g):
    ms = jnp.mean(jnp.square(xf), axis=-1, keepdims=True)
    return xf * lax.rsqrt(ms + NORM_EPS) * g


def _rope(x_f32, x_b16, r_ref, c, s):
    return x_f32 * c + _dot(x_b16, r_ref[...]) * s


def _store_vt(vt_ref, v, heads):
    vt = v.T.astype(BF16)
    ones = jnp.ones((BF16_ROWS, v.shape[0]), BF16)
    for h in range(heads):
        vt_ref[h * VT_ROWS:h * VT_ROWS + LANES, :] = vt[h * LANES:(h + 1) * LANES]
        vt_ref[h * VT_ROWS + LANES:(h + 1) * VT_ROWS, :] = ones


def _prep_kernel(dq_ref, dk_ref, dv_ref, cq_ref, ckv_ref, ps_ref, cd_ref, sd_ref, cmq_ref, smq_ref, ckp_ref, skp_ref,
                 gq_ref, gkv_ref, wuq_ref, wukv_ref, rd_ref, rmq_ref, rkp_ref, bf_ref,
                 dqo_ref, dko_ref, dvt_ref, mq_ref, mk_ref, mvt_ref, lft_ref):
    cd = cd_ref[...]
    sd = sd_ref[...]
    dq = dq_ref[...]
    dk = dk_ref[...]
    dqo_ref[...] = (_rope(dq.astype(F32), dq, rd_ref, cd, sd) * (DIFF_HEAD_DIM ** -0.5 * LOG2E)).astype(BF16)
    dko_ref[...] = _rope(dk.astype(F32), dk, rd_ref, cd, sd).astype(BF16)
    _store_vt(dvt_ref, dv_ref[...].astype(F32), DIFF_HEADS)

    cqn = _rms(cq_ref[...].astype(F32), gq_ref[...]).astype(BF16)
    q = _dot(cqn, wuq_ref[...])
    mq_ref[...] = _rope(q, q.astype(BF16), rmq_ref, cmq_ref[...], smq_ref[...]).astype(BF16)

    ckvn = _rms(ckv_ref[...].astype(F32), gkv_ref[...]).astype(BF16)
    kv = _dot(ckvn, wukv_ref[...])
    ps = ps_ref[...]
    kpe = _rope(ps, ps.astype(BF16), rkp_ref, ckp_ref[...], skp_ref[...]).astype(BF16)
    for h in range(MLA_HEADS):
        mk_ref[:, h * MLA_QK_PAD:h * MLA_QK_PAD + LANES] = kv[:, h * LANES:(h + 1) * LANES].astype(BF16)
        mk_ref[:, h * MLA_QK_PAD + LANES:(h + 1) * MLA_QK_PAD] = kpe
    _store_vt(mvt_ref, kv[:, MLA_HEADS * LANES:], MLA_HEADS)

    z = ps + bf_ref[...]
    lf = jnp.minimum(z, 0.0) - jnp.log1p(jnp.exp(-jnp.abs(z)))
    lft_ref[...] = lf.T[MLA_ROPE_DIM:MLA_ROPE_DIM + 8, :]


def _prep(pa, ps, tabs, gq, gkv, wuq, wukv, rmats, bf):
    s = pa.shape[0]
    tm = PREP_TM
    row = lambda w: pl.BlockSpec((tm, w), lambda i: (i, 0))
    col = lambda w, c: pl.BlockSpec((tm, w), lambda i: (i, c))
    tr = lambda r: pl.BlockSpec((r, tm), lambda i: (0, i))
    full = lambda a: pl.BlockSpec(a.shape, lambda i: (0,) * a.ndim)
    cd, sd, cmq, smq, ckp, skp = tabs
    rd, rmq, rkp = rmats
    return pl.pallas_call(
        _prep_kernel,
        grid=(s // tm,),
        in_specs=[col(512, 3), col(512, 4), col(512, 5), col(512, 6), col(512, 7), row(LANES),
                  row(512), row(512), row(1024), row(1024), row(LANES), row(LANES),
                  full(gq), full(gkv), full(wuq), full(wukv), full(rd), full(rmq), full(rkp), full(bf)],
        out_specs=[row(512), row(512), tr(4 * VT_ROWS), row(1024), row(1024), tr(4 * VT_ROWS), tr(8)],
        out_shape=[jax.ShapeDtypeStruct((s, 512), BF16), jax.ShapeDtypeStruct((s, 512), BF16),
                   jax.ShapeDtypeStruct((4 * VT_ROWS, s), BF16),
                   jax.ShapeDtypeStruct((s, 1024), BF16), jax.ShapeDtypeStruct((s, 1024), BF16),
                   jax.ShapeDtypeStruct((4 * VT_ROWS, s), BF16), jax.ShapeDtypeStruct((8, s), F32)],
        compiler_params=_cparams(1),
        name="prep",
    )(pa, pa, pa, pa, pa, ps, cd, sd, cmq, smq, ckp, skp, gq, gkv, wuq, wukv, rd, rmq, rkp, bf)


def _cumsum_kernel(x_ref, u_ref, o_ref, carry_ref):
    @pl.when(pl.program_id(0) == 0)
    def _():
        carry_ref[...] = jnp.zeros_like(carry_ref)

    a, b, c = _split3(x_ref[...])
    u = u_ref[...]
    cs = _dot(a, u) + _dot(b, u) + _dot(c, u) + carry_ref[:, :1]
    o_ref[...] = cs
    carry_ref[...] = jnp.broadcast_to(cs[:, CUM_T - 1:CUM_T], carry_ref.shape)


def _cumsum(x, u):
    s = x.shape[1]
    return pl.pallas_call(
        _cumsum_kernel,
        grid=(s // CUM_T,),
        in_specs=[pl.BlockSpec((8, CUM_T), lambda i: (0, i)), pl.BlockSpec(u.shape, lambda i: (0, 0))],
        out_specs=pl.BlockSpec((8, CUM_T), lambda i: (0, i)),
        out_shape=jax.ShapeDtypeStruct((8, s), F32),
        scratch_shapes=[pltpu.VMEM((8, LANES), F32)],
        compiler_params=_cparams(1),
        name="cumsum",
    )(x, u)


N_BIAS = 3


def _foxprep_kernel(q_ref, k_ref, v_ref, cum_ref, qo_ref, ko_ref, vt_ref):
    cum = cum_ref[...]
    beta = (cum[:, 0:1] - cum) * LOG2E
    pieces = [p.astype(F32).T for p in _split3(beta)]
    lane = lax.broadcasted_iota(jnp.int32, (ATT_T, LANES), 1)
    ones = jnp.where(lane < N_BIAS, 1.0, 0.0).astype(BF16)
    q = q_ref[...].astype(F32) * (FOX_HEAD_DIM ** -0.5 * LOG2E)
    for h in range(FOX_HEADS):
        e = jnp.zeros((ATT_T, LANES), F32)
        for n, p in enumerate(pieces):
            e = jnp.where(lane == n, p[:, h:h + 1], e)
        ko_ref[:, h * FOX_QK_PAD:h * FOX_QK_PAD + LANES] = k_ref[:, h * LANES:(h + 1) * LANES]
        ko_ref[:, h * FOX_QK_PAD + LANES:(h + 1) * FOX_QK_PAD] = e.astype(BF16)
        qo_ref[:, h * FOX_QK_PAD:h * FOX_QK_PAD + LANES] = q[:, h * LANES:(h + 1) * LANES].astype(BF16)
        qo_ref[:, h * FOX_QK_PAD + LANES:(h + 1) * FOX_QK_PAD] = ones
    _store_vt(vt_ref, v_ref[...].astype(F32), FOX_HEADS)


def _foxprep(pa, cum):
    s = pa.shape[0]
    t = ATT_T
    w = FOX_HEADS * FOX_QK_PAD
    return pl.pallas_call(
        _foxprep_kernel,
        grid=(s // t,),
        in_specs=[pl.BlockSpec((t, 512), lambda i: (i, 0)), pl.BlockSpec((t, 512), lambda i: (i, 1)),
                  pl.BlockSpec((t, 512), lambda i: (i, 2)), pl.BlockSpec((8, t), lambda i: (0, i))],
        out_specs=[pl.BlockSpec((t, w), lambda i: (i, 0)), pl.BlockSpec((t, w), lambda i: (i, 0)),
                   pl.BlockSpec((FOX_HEADS * VT_ROWS, t), lambda i: (0, i))],
        out_shape=[jax.ShapeDtypeStruct((s, w), BF16), jax.ShapeDtypeStruct((s, w), BF16),
                   jax.ShapeDtypeStruct((FOX_HEADS * VT_ROWS, s), BF16)],
        compiler_params=_cparams(1),
        name="foxprep",
    )(pa, pa, pa, cum)


def _flash_kernel(*refs, mode, lam_init):
    t = ATT_T
    if mode == "fox":
        q_ref, k_ref, vt_ref, ck_ref, o_ref, sa_ref, sb_ref, xa_ref, xb_ref, m_ref, acc_ref = refs
    elif mode == "diff":
        q_ref, k_ref, vt_ref, lam_ref, gd_ref, o_ref, sa_ref, sb_ref, xa_ref, xb_ref, m_ref, acc_ref = refs
    else:
        q_ref, k_ref, vt_ref, o_ref, sa_ref, sb_ref, xa_ref, xb_ref, m_ref, acc_ref = refs
    i = pl.program_id(1)
    off_i = pl.multiple_of(i * ATT_TQ, ATT_TQ)
    m_ref[...] = jnp.full(m_ref.shape, NEG, F32)
    acc_ref[...] = jnp.zeros_like(acc_ref)

    q = q_ref[...]
    if mode == "diff":
        lane = lax.broadcasted_iota(jnp.int32, q.shape, 1)
        zero = jnp.zeros_like(q)
        qs = [jnp.where(lane < DIFF_HEAD_DIM, q, zero), jnp.where(lane >= DIFF_HEAD_DIM, q, zero)]
    else:
        qs = [q]

    buf_a = (sa_ref, xa_ref)
    buf_b = (sb_ref, xb_ref)

    def scores(off, buf, q0=0):
        s_ref, x_ref = buf
        w = ATT_TQ - q0
        kb = k_ref[pl.ds(off, t), :]
        for n, qn in enumerate(qs):
            st = _dot_nt(kb, qn[q0:])
            s_ref[n, :, :w] = st
            x_ref[n, :, :w] = jnp.max(st, axis=0, keepdims=True)

    def consume(off, buf, masked, q0=0):
        s_ref, x_ref = buf
        w = ATT_TQ - q0
        vtb = vt_ref[:, pl.ds(off, t)]
        for n in range(len(qs)):
            st = s_ref[n, :, :w]
            if masked:
                key_id = lax.broadcasted_iota(jnp.int32, st.shape, 0) + (off - off_i)
                qry_id = lax.broadcasted_iota(jnp.int32, st.shape, 1) + q0
                st = jnp.where(key_id <= qry_id, st, NEG)
                m = jnp.max(st, axis=0, keepdims=True)
            else:
                m = x_ref[n, :, :w]
            p = jnp.exp2((st - m).astype(BF16))
            o = _dot(vtb, p)
            if mode == "fox":
                m = m + (ck_ref[:, pl.ds(off_i, LANES)][:, :1] - ck_ref[:, pl.ds(off, LANES)][:, :1]) * LOG2E
            m_prev = m_ref[n, :, q0:]
            m_new = jnp.maximum(m_prev, m)
            acc_ref[n, :, q0:] = jnp.exp2(m_prev - m_new) * acc_ref[n, :, q0:] + jnp.exp2(m - m_new) * o
            m_ref[n, :, q0:] = m_new

    scores(0, buf_a)

    def pair(jj, carry):
        off0 = pl.multiple_of(jj * (2 * t), 2 * t)
        off1 = pl.multiple_of(off0 + t, t)
        scores(off1, buf_b)
        consume(off0, buf_a, False)
        scores(pl.multiple_of(off1 + t, t), buf_a)
        consume(off1, buf_b, False)
        return carry

    lax.fori_loop(0, i, pair, 0)
    off_d = pl.multiple_of(off_i + t, t)
    scores(off_d, buf_b, q0=t)
    consume(off_i, buf_a, True)
    consume(off_d, buf_b, True, q0=t)

    def normalised(n):
        acc = acc_ref[n]
        return acc[:LANES] / acc[LANES:LANES + 1]

    if mode == "diff":
        dl = lam_ref[...]
        lam = (jnp.exp(jnp.sum(dl[0:1] * dl[1:2], axis=1, keepdims=True))
               - jnp.exp(jnp.sum(dl[2:3] * dl[3:4], axis=1, keepdims=True)) + lam_init)
        o = _rms((normalised(0) - lam * normalised(1)).T, gd_ref[...]) * (1.0 - lam_init)
    else:
        o = normalised(0).T
    o_ref[...] = o.astype(o_ref.dtype)


def _flash(mode, q_arr, k_arr, vt_arr, dk, heads, extra=(), lam_init=0.0):
    s = q_arr.shape[0]
    t = ATT_T
    tq = ATT_TQ
    dv = LANES
    n_stream = 2 if mode == "diff" else 1
    in_specs = [pl.BlockSpec((tq, dk), lambda h, i: (i, h)),
                pl.BlockSpec((s, dk), lambda h, i: (0, h)),
                pl.BlockSpec((VT_ROWS, s), lambda h, i: (h, 0))]
    if mode == "fox":
        in_specs.append(pl.BlockSpec((None, 1, s), lambda h, i: (h, 0, 0)))
    elif mode == "diff":
        in_specs += [pl.BlockSpec(extra[0].shape, lambda h, i: (0, 0)),
                     pl.BlockSpec(extra[1].shape, lambda h, i: (0, 0))]
    return pl.pallas_call(
        functools.partial(_flash_kernel, mode=mode, lam_init=lam_init),
        grid=(heads, s // tq),
        in_specs=in_specs,
        out_specs=pl.BlockSpec((tq, dv), lambda h, i: (i, h)),
        out_shape=jax.ShapeDtypeStruct((s, heads * dv), BF16),
        scratch_shapes=[pltpu.VMEM((n_stream, t, tq), F32), pltpu.VMEM((n_stream, t, tq), F32),
                        pltpu.VMEM((n_stream, 1, tq), F32), pltpu.VMEM((n_stream, 1, tq), F32),
                        pltpu.VMEM((n_stream, 1, tq), F32), pltpu.VMEM((n_stream, VT_ROWS, tq), F32)],
        compiler_params=_cparams(2),
        name="flash_" + mode,
    )(q_arr, k_arr, vt_arr, *extra)


def _layer_norm(z, g, b):
    mu = jnp.mean(z, axis=-1, keepdims=True)
    zc = z - mu
    var = jnp.mean(jnp.square(zc), axis=-1, keepdims=True)
    return zc * lax.rsqrt(var + NORM_EPS) * g + b


def _first_index_of_max(vals):
    best = vals[0]
    for v in vals[1:]:
        best = jnp.maximum(best, v)
    idx = jnp.full(best.shape, len(vals) - 1, jnp.int32)
    for j in range(len(vals) - 2, -1, -1):
        idx = jnp.where(vals[j] == best, j, idx)
    return best, idx


def _post_kernel(of_ref, od_ref, om_ref, g0_ref, g1_ref, g2_ref, x_ref, wf_ref, wd_ref, wm_ref, wo_ref,
                 lg_ref, lb_ref, wr_ref, rb_ref, x1_ref, h16_ref, eid_ref, wt_ref):
    tm = POST_TM
    merged = (g0_ref[...].astype(F32) * _dot(of_ref[...], wf_ref[...])
              + g1_ref[...].astype(F32) * _dot(od_ref[...], wd_ref[...])
              + g2_ref[...].astype(F32) * _dot(om_ref[...], wm_ref[...]))
    mix = _dot(merged.astype(BF16), wo_ref[...])
    x1 = _layer_norm(DEEPNORM_ALPHA * x_ref[...] + mix, lg_ref[...], lb_ref[...])
    x1_ref[...] = x1
    for c in range(ROW_CHUNKS):
        h16_ref[pl.ds(c, tm, stride=ROW_CHUNKS), :] = x1[:, c * LANES:(c + 1) * LANES]

    e = N_EXPERTS
    prod = [_dot(p, wr_ref[...]).T for p in _split3(x1)]
    logits = (prod[0][0:e] + (prod[0][e:2 * e] + prod[1][0:e])
              + (prod[0][2 * e:3 * e] + prod[1][e:2 * e] + prod[2][0:e]))
    scores = jax.nn.sigmoid(logits)
    biased = scores + rb_ref[...]
    a = [biased[j * 8:(j + 1) * 8] for j in range(EXPERTS_PER_GROUP)]
    u = [scores[j * 8:(j + 1) * 8] for j in range(EXPERTS_PER_GROUP)]
    gs = a[0] + a[1]
    for j0 in range(EXPERTS_PER_GROUP):
        for j1 in range(j0 + 1, EXPERTS_PER_GROUP):
            gs = jnp.maximum(gs, a[j0] + a[j1])
    gmax = jnp.max(gs, axis=0, keepdims=True)
    gid = lax.broadcasted_iota(jnp.int32, gs.shape, 0)
    best = jnp.min(jnp.where(gs == gmax, gid, N_GROUPS), axis=0, keepdims=True)
    sel = gid == best
    ing = [jnp.sum(jnp.where(sel, a[j], 0.0), axis=0, keepdims=True) for j in range(EXPERTS_PER_GROUP)]
    unb = [jnp.sum(jnp.where(sel, u[j], 0.0), axis=0, keepdims=True) for j in range(EXPERTS_PER_GROUP)]
    _, l1 = _first_index_of_max(ing)
    ing2 = [jnp.where(l1 == j, -jnp.inf, ing[j]) for j in range(EXPERTS_PER_GROUP)]
    _, l2 = _first_index_of_max(ing2)
    w1 = jnp.zeros_like(unb[0])
    w2 = jnp.zeros_like(unb[0])
    for j in range(EXPERTS_PER_GROUP):
        w1 = jnp.where(l1 == j, unb[j], w1)
        w2 = jnp.where(l2 == j, unb[j], w2)
    tot = w1 + w2
    eid_ref[...] = jnp.zeros_like(eid_ref)
    wt_ref[...] = jnp.zeros_like(wt_ref)
    eid_ref[0:1, :] = best * EXPERTS_PER_GROUP + l1
    eid_ref[1:2, :] = best * EXPERTS_PER_GROUP + l2
    wt_ref[0:1, :] = w1 / tot
    wt_ref[1:2, :] = w2 / tot


def _post(o_fox, o_diff, o_mla, gates, x, wf, wd, wm, wo, lg, lb, wr, rb):
    s = x.shape[0]
    tm = POST_TM
    row = lambda w: pl.BlockSpec((tm, w), lambda i: (i, 0))
    full = lambda a: pl.BlockSpec(a.shape, lambda i: (0,) * a.ndim, pipeline_mode=pl.Buffered(1))
    return pl.pallas_call(
        _post_kernel,
        grid=(s // tm,),
        in_specs=[row(512), row(512), row(512),
                  pl.BlockSpec((tm, D_MODEL), lambda i: (i, 0)),
                  pl.BlockSpec((tm, D_MODEL), lambda i: (i, 1)),
                  pl.BlockSpec((tm, D_MODEL), lambda i: (i, 2)),
                  row(D_MODEL), full(wf), full(wd), full(wm), full(wo), full(lg), full(lb), full(wr), full(rb)],
        out_specs=[row(D_MODEL), pl.BlockSpec((tm * ROW_CHUNKS, LANES), lambda i: (i, 0)),
                   pl.BlockSpec((8, tm), lambda i: (0, i)), pl.BlockSpec((8, tm), lambda i: (0, i))],
        out_shape=[jax.ShapeDtypeStruct((s, D_MODEL), F32),
                   jax.ShapeDtypeStruct((s * ROW_CHUNKS, LANES), F32),
                   jax.ShapeDtypeStruct((8, s), jnp.int32), jax.ShapeDtypeStruct((8, s), F32)],
        compiler_params=_cparams(1),
        name="post",
    )(o_fox, o_diff, o_mla, gates, gates, gates, x, wf, wd, wm, wo, lg, lb, wr, rb)


def _rank_kernel(eid_ref, us_ref, rank_ref, cnt_ref, carry_ref):
    @pl.when(pl.program_id(0) == 0)
    def _():
        carry_ref[...] = jnp.zeros_like(carry_ref)

    e0 = eid_ref[0:1, :]
    e1 = eid_ref[1:2, :]
    eio = lax.broadcasted_iota(jnp.int32, (N_EXPERTS, RANK_T), 0)
    hit0 = eio == e0
    hit1 = eio == e1
    onehot = jnp.where(hit0 | hit1, 1.0, 0.0)
    before = _dot(onehot.astype(BF16), us_ref[...]) + carry_ref[:, :1]
    rank_ref[...] = jnp.zeros_like(rank_ref)
    rank_ref[0:1, :] = jnp.sum(jnp.where(hit0, before, 0.0), axis=0, keepdims=True).astype(jnp.int32)
    rank_ref[1:2, :] = jnp.sum(jnp.where(hit1, before, 0.0), axis=0, keepdims=True).astype(jnp.int32)
    carry_ref[...] = carry_ref[...] + jnp.sum(onehot, axis=1, keepdims=True)
    cnt_ref[...] = carry_ref[...]


def _rank(eid, us):
    s = eid.shape[1]
    return pl.pallas_call(
        _rank_kernel,
        grid=(s // RANK_T,),
        in_specs=[pl.BlockSpec((8, RANK_T), lambda i: (0, i)), pl.BlockSpec(us.shape, lambda i: (0, 0))],
        out_specs=[pl.BlockSpec((8, RANK_T), lambda i: (0, i)),
                   pl.BlockSpec((N_EXPERTS, LANES), lambda i: (0, 0))],
        out_shape=[jax.ShapeDtypeStruct((8, s), jnp.int32), jax.ShapeDtypeStruct((N_EXPERTS, LANES), F32)],
        scratch_shapes=[pltpu.VMEM((N_EXPERTS, LANES), F32)],
        compiler_params=_cparams(1),
        name="rank",
    )(eid, us)


def _dest_kernel(cnt_ref, eid_ref, rank_ref, lt_ref, dest_ref, blk_ref):
    nb = jnp.floor((cnt_ref[...] + (MOE_R - 1)) * (1.0 / MOE_R))
    incl = _dot(lt_ref[...], nb.astype(BF16))
    start = (incl - nb)[:, :1] * MOE_R
    eio = lax.broadcasted_iota(jnp.int32, (N_EXPERTS, DEST_T), 0)
    dest_ref[...] = jnp.zeros_like(dest_ref)
    for k in range(TOP_K):
        base = jnp.sum(jnp.where(eio == eid_ref[k:k + 1, :], start, 0.0), axis=0, keepdims=True)
        dest_ref[k:k + 1, :] = base.astype(jnp.int32) + rank_ref[k:k + 1, :]
    bid = lax.broadcasted_iota(jnp.int32, (N_EXPERTS, BLK_LANES), 1).astype(F32)
    ended = jnp.sum(jnp.where(incl[:, :1] <= bid, 1.0, 0.0), axis=0, keepdims=True)
    blk_ref[...] = jnp.zeros_like(blk_ref)
    blk_ref[0:1, :] = jnp.minimum(ended, N_EXPERTS - 1.0).astype(jnp.int32)
    blk_ref[1:2, :] = jnp.broadcast_to(incl[N_EXPERTS - 1:N_EXPERTS, :1], (1, BLK_LANES)).astype(jnp.int32)


def _dest(cnt, eid, rank, lt):
    s = eid.shape[1]
    return pl.pallas_call(
        _dest_kernel,
        grid=(s // DEST_T,),
        in_specs=[pl.BlockSpec(cnt.shape, lambda i: (0, 0)),
                  pl.BlockSpec((8, DEST_T), lambda i: (0, i)), pl.BlockSpec((8, DEST_T), lambda i: (0, i)),
                  pl.BlockSpec(lt.shape, lambda i: (0, 0))],
        out_specs=[pl.BlockSpec((8, DEST_T), lambda i: (0, i)), pl.BlockSpec((8, BLK_LANES), lambda i: (0, 0))],
        out_shape=[jax.ShapeDtypeStruct((8, s), jnp.int32), jax.ShapeDtypeStruct((8, BLK_LANES), jnp.int32)],
        compiler_params=_cparams(1),
        name="dest",
    )(cnt, eid, rank, lt)


def _row_copy(src_ref, src_row, dst_ref, dst_row, sem):
    def rows(row):
        start = row * ROW_CHUNKS
        return pl.ds(start if isinstance(row, int) else pl.multiple_of(start, ROW_CHUNKS), ROW_CHUNKS)

    return pltpu.make_async_copy(src_ref.at[rows(src_row)], dst_ref.at[rows(dst_row)], sem)


def _dispatch_kernel(dest_ref, h_ref, xs_in_ref, xs_ref, sem):
    del xs_in_ref
    base = pl.program_id(0) * DISP_T

    def issue(r, carry):
        for k in range(TOP_K):
            _row_copy(h_ref, r, xs_ref, dest_ref[k * SEQ + base + r], sem).start(priority=k)
        return carry

    lax.fori_loop(0, DISP_T, issue, 0, unroll=ISSUE_UNROLL)
    for k in range(TOP_K):
        pltpu.make_async_copy(h_ref, xs_ref.at[pl.ds(0, DISP_T * ROW_CHUNKS)], sem).wait()


def _dispatch(dest_flat, h16, xs_zero):
    return pl.pallas_call(
        _dispatch_kernel,
        grid_spec=pltpu.PrefetchScalarGridSpec(
            num_scalar_prefetch=1, grid=(SEQ // DISP_T,),
            in_specs=[pl.BlockSpec((DISP_T * ROW_CHUNKS, LANES), lambda i, d: (i, 0)),
                      pl.BlockSpec(memory_space=pl.ANY)],
            out_specs=pl.BlockSpec(memory_space=pl.ANY),
            scratch_shapes=[pltpu.SemaphoreType.DMA(())]),
        out_shape=jax.ShapeDtypeStruct(xs_zero.shape, xs_zero.dtype),
        input_output_aliases={2: 0},
        compiler_params=_cparams(1),
        name="dispatch",
    )(dest_flat, h16, xs_zero)


def _moe_kernel(be_ref, nu_ref, xs_ref, wgu_ref, wout_ref, y_ref, lhs_ref):
    del be_ref
    used = pl.program_id(0) < nu_ref[0]

    @pl.when(jnp.logical_not(used))
    def _():
        y_ref[...] = jnp.zeros_like(y_ref)

    @pl.when(used)
    def _():
        for c in range(ROW_CHUNKS):
            lhs_ref[:, c * LANES:(c + 1) * LANES] = xs_ref[pl.ds(c, MOE_R, stride=ROW_CHUNKS), :].astype(BF16)
        gu = _dot(lhs_ref[...], wgu_ref[...])
        hmid = (jax.nn.silu(gu[:, :D_FF_EXPERT]) * gu[:, D_FF_EXPERT:]).astype(BF16)
        y = _dot(hmid, wout_ref[...])
        for c in range(ROW_CHUNKS):
            y_ref[pl.ds(c, MOE_R, stride=ROW_CHUNKS), :] = y[:, c * LANES:(c + 1) * LANES]


def _moe(blk_e, n_used, xs, wgu, wout, layer):
    last = lambda b, nu: jnp.minimum(b, nu[0] - 1)
    rows = pl.BlockSpec((MOE_R * ROW_CHUNKS, LANES), lambda b, be, nu: (last(b, nu), 0))
    return pl.pallas_call(
        _moe_kernel,
        grid_spec=pltpu.PrefetchScalarGridSpec(
            num_scalar_prefetch=2, grid=(MOE_NB,),
            in_specs=[rows,
                      pl.BlockSpec((None, None) + wgu.shape[2:], lambda b, be, nu: (layer, be[last(b, nu)], 0, 0)),
                      pl.BlockSpec((None, None) + wout.shape[2:], lambda b, be, nu: (layer, be[last(b, nu)], 0, 0))],
            out_specs=pl.BlockSpec((MOE_R * ROW_CHUNKS, LANES), lambda b, be, nu: (b, 0)),
            scratch_shapes=[pltpu.VMEM((MOE_R, D_MODEL), BF16)]),
        out_shape=jax.ShapeDtypeStruct(xs.shape, F32),
        compiler_params=_cparams(1),
        name="moe",
    )(blk_e, n_used, xs, wgu, wout)


def _combine_kernel(dest_ref, y_ref, x1_ref, wt_ref, g_ref, b_ref, o_ref, buf_ref, sem):
    i = pl.program_id(0)
    slot = i & 1

    def issue(tile, sl):
        def body(r, carry):
            for k in range(TOP_K):
                _row_copy(y_ref, dest_ref[k * SEQ + tile * COMB_T + r], buf_ref.at[sl, k], r,
                          sem.at[sl]).start(priority=k)
            return carry

        lax.fori_loop(0, COMB_T, body, 0, unroll=ISSUE_UNROLL)

    @pl.when(i == 0)
    def _():
        issue(0, 0)

    @pl.when(i + 1 < pl.num_programs(0))
    def _():
        issue(i + 1, 1 - slot)

    for k in range(TOP_K):
        pltpu.make_async_copy(y_ref.at[pl.ds(0, COMB_T * ROW_CHUNKS)], buf_ref.at[slot, k], sem.at[slot]).wait()

    wcol = wt_ref[...].T
    w0 = wcol[:, 0:1]
    w1 = wcol[:, 1:2]
    ffn = jnp.concatenate(
        [w0 * buf_ref[slot, 0, pl.ds(c, COMB_T, stride=ROW_CHUNKS), :]
         + w1 * buf_ref[slot, 1, pl.ds(c, COMB_T, stride=ROW_CHUNKS), :] for c in range(ROW_CHUNKS)], axis=1)
    o_ref[...] = _layer_norm(DEEPNORM_ALPHA * x1_ref[...] + ffn, g_ref[...], b_ref[...])


def _combine(dest_flat, y16, x1, wt, g, b):
    tm = COMB_T
    return pl.pallas_call(
        _combine_kernel,
        grid_spec=pltpu.PrefetchScalarGridSpec(
            num_scalar_prefetch=1, grid=(SEQ // tm,),
            in_specs=[pl.BlockSpec(memory_space=pl.ANY),
                      pl.BlockSpec((tm, D_MODEL), lambda i, d: (i, 0)),
                      pl.BlockSpec((8, tm), lambda i, d: (0, i)),
                      pl.BlockSpec(g.shape, lambda i, d: (0, 0)), pl.BlockSpec(b.shape, lambda i, d: (0, 0))],
            out_specs=pl.BlockSpec((tm, D_MODEL), lambda i, d: (i, 0)),
            scratch_shapes=[pltpu.VMEM((2, TOP_K, tm * ROW_CHUNKS, LANES), F32), pltpu.SemaphoreType.DMA((2,))]),
        out_shape=jax.ShapeDtypeStruct(x1.shape, F32),
        compiler_params=_cparams(1),
        name="combine",
    )(dest_flat, y16, x1, wt, g, b)


def _rot_matrix(width, groups, half):
    r = np.zeros((width, width), np.float32)
    for o in groups:
        for j in range(half):
            r[o + j + half, o + j] = -1.0
            r[o + j, o + half + j] = 1.0
    return jnp.asarray(r, BF16)


def _rope_tables():
    pos = jnp.arange(SEQ, dtype=F32)

    def cs(dim):
        inv_freq = ROPE_THETA ** (-jnp.arange(0, dim, 2, dtype=F32) / dim)
        ang = pos[:, None] * inv_freq[None, :]
        return jnp.cos(ang), jnp.sin(ang)

    cp, sp = cs(DIFF_ROT_DIM)
    one = jnp.ones((SEQ, DIFF_HEAD_DIM - DIFF_ROT_DIM), F32)
    cd = jnp.tile(jnp.concatenate([cp, cp, one], axis=1), (1, 2 * DIFF_HEADS))
    sd = jnp.tile(jnp.concatenate([sp, sp, 0.0 * one], axis=1), (1, 2 * DIFF_HEADS))
    cm, sm = cs(MLA_ROPE_DIM)
    scale = MLA_QK_DIM ** -0.5 * LOG2E
    ones_n = jnp.ones((SEQ, MLA_NOPE_DIM), F32)
    pad = jnp.zeros((SEQ, MLA_QK_PAD - MLA_QK_DIM), F32)
    cmq = jnp.tile(jnp.concatenate([ones_n, cm, cm, pad], axis=1) * scale, (1, MLA_HEADS))
    smq = jnp.tile(jnp.concatenate([0.0 * ones_n, sm, sm, pad], axis=1) * scale, (1, MLA_HEADS))
    ckp = jnp.concatenate([cm, cm, pad], axis=1)
    skp = jnp.concatenate([sm, sm, pad], axis=1)
    return cd, sd, cmq, smq, ckp, skp


def _constants():
    rd = _rot_matrix(512, [c * DIFF_HEAD_DIM for c in range(2 * DIFF_HEADS)], DIFF_ROT_DIM // 2)
    rmq = _rot_matrix(MLA_HEADS * MLA_QK_PAD, [h * MLA_QK_PAD + MLA_NOPE_DIM for h in range(MLA_HEADS)],
                      MLA_ROPE_DIM // 2)
    rkp = _rot_matrix(LANES, [0], MLA_ROPE_DIM // 2)
    tri = np.arange(CUM_T)
    u_incl = jnp.asarray(tri[:, None] <= tri[None, :], BF16)
    tri = np.arange(RANK_T)
    u_strict = jnp.asarray(tri[:, None] < tri[None, :], BF16)
    tri = np.arange(N_EXPERTS)
    l_incl = jnp.asarray(tri[:, None] >= tri[None, :], BF16)
    return (rd, rmq, rkp), u_incl, u_strict, l_incl


def kernel(x, w_in, b_fox_f, b_gate, diff_lambda, g_diff, g_mla_q, g_mla_kv, w_mla_uq, w_mla_ukv, w_fox_up, w_diff_up, w_mla_up, w_o, ln1_g, ln1_b, ln2_g, ln2_b, w_router, router_bias, w_exp_in, w_exp_out):
    s = x.shape[1]
    assert x.shape == (1, SEQ, D_MODEL) and w_in.shape == (DEPTH, D_MODEL, IN_WIDTH)
    xf = x.reshape(s, D_MODEL)
    tabs = _rope_tables()
    rmats, u_incl, u_strict, l_incl = _constants()
    w_ag, w_s = _repack(w_in)

    perm = np.array([g * EXPERTS_PER_GROUP + j for j in range(EXPERTS_PER_GROUP) for g in range(N_GROUPS)])
    wr = jnp.concatenate(list(_split3(w_router[:, perm]))
                         + [jnp.zeros((D_MODEL, LANES - 3 * N_EXPERTS), BF16)], axis=1)
    rb = router_bias[perm].reshape(N_EXPERTS, 1)
    wgu = w_exp_in.astype(BF16)
    wout = w_exp_out.astype(BF16)
    zero_s = jnp.zeros((1, LANES), F32)

    for i in range(DEPTH):
        pa, gates = _inproj(xf, w_ag, i, b_gate[i].reshape(1, -1))
        ps = _linear(xf, w_s, i, zero_s, None, F32, LANES, "inproj_small")

        uq = w_mla_uq[i].reshape(MLA_Q_RANK, MLA_HEADS, MLA_QK_DIM)
        wuq = jnp.concatenate([uq, jnp.zeros((MLA_Q_RANK, MLA_HEADS, MLA_QK_PAD - MLA_QK_DIM), F32)], axis=2)
        wuq = wuq.reshape(MLA_Q_RANK, MLA_HEADS * MLA_QK_PAD).astype(BF16)
        ukv = w_mla_ukv[i].reshape(MLA_KV_RANK, MLA_HEADS, MLA_NOPE_DIM + MLA_V_DIM)
        wukv = jnp.concatenate([ukv[:, :, :MLA_NOPE_DIM].reshape(MLA_KV_RANK, -1),
                                ukv[:, :, MLA_NOPE_DIM:].reshape(MLA_KV_RANK, -1)], axis=1).astype(BF16)
        bf = jnp.zeros((1, LANES), F32).at[0, MLA_ROPE_DIM:MLA_ROPE_DIM + FOX_HEADS].set(b_fox_f[i])

        dq, dk, dvt, mq, mk, mvt, lft = _prep(pa, ps, tabs, g_mla_q[i].reshape(1, -1), g_mla_kv[i].reshape(1, -1),
                                              wuq, wukv, rmats, bf)
        cum = _cumsum(lft, u_incl)
        fq, fk, fvt = _foxprep(pa, cum)
        cum_row = cum[:FOX_HEADS].reshape(FOX_HEADS, 1, s)

        lam_init = 0.8 - 0.6 * math.exp(-0.3 * i)
        o_fox = _flash("fox", fq, fk, fvt, FOX_QK_PAD, FOX_HEADS, extra=(cum_row,))
        o_diff = _flash("diff", dq, dk, dvt, 2 * DIFF_HEAD_DIM, DIFF_HEADS,
                        extra=(diff_lambda[i], g_diff[i].reshape(1, -1)), lam_init=lam_init)
        o_mla = _flash("mla", mq, mk, mvt, MLA_QK_PAD, MLA_HEADS)

        x1, h16, eid, wt = _post(o_fox, o_diff, o_mla, gates, xf,
                                 w_fox_up[i].astype(BF16), w_diff_up[i].astype(BF16), w_mla_up[i].astype(BF16),
                                 w_o[i].astype(BF16), ln1_g[i].reshape(1, -1), ln1_b[i].reshape(1, -1), wr, rb)

        rank, cnt = _rank(eid, u_strict)
        dest, blk = _dest(cnt, eid, rank, l_incl)
        dest_flat = dest[:TOP_K].reshape(TOP_K * s)
        xs = _dispatch(dest_flat, h16, jnp.zeros((MOE_ROWS * ROW_CHUNKS, LANES), F32))
        y16 = _moe(blk[0], blk[1, :1], xs, wgu, wout, i)
        xf = _combine(dest_flat, y16, x1, wt, ln2_g[i].reshape(1, -1), ln2_b[i].reshape(1, -1))
    return xf.reshape(1, s, D_MODEL)
```
